```python
import jax, jax.numpy as jnp
from jax import lax
import numpy as np


D_MODEL = 1024
BATCH = 8
SEQ = 4096
DEPTH = 2

D_MIX = D_MODEL
NSA_HEADS = 6
NSA_KV_HEADS = 2
NSA_HPG = NSA_HEADS // NSA_KV_HEADS
NSA_DH = 64
NSA_CMP_LEN = 32
NSA_CMP_STRIDE = 16
NSA_SLC_LEN = 64
NSA_TOPN = 16
NSA_WINDOW = 512
NSA_Q_BLOCK = 64
NSA_FORCE = 1.0e4
MLA_HEADS = 6
MLA_Q_LORA = 256
MLA_KV_LORA = 128
MLA_NOPE = 64
MLA_ROPE = 32
MLA_VDIM = 64
MLA_Q_BLOCK = 128
ROPE_THETA = 10000.0
POOL_WINDOWS = (2, 4, 8, 16)
POOL_GROUP = 64
POOL_WIDTH = 4 * POOL_GROUP
D_FF = 4 * D_MODEL
EPS = 1e-6

NSA_Q_W = NSA_HEADS * NSA_DH
NSA_KV_W = 3 * 2 * NSA_KV_HEADS * NSA_DH
NSA_GATE_W = 3 * NSA_HEADS
IN_SPLITS = (NSA_Q_W, NSA_KV_W, NSA_GATE_W, MLA_Q_LORA, MLA_KV_LORA, MLA_ROPE, POOL_WIDTH)
P_IN = 384 + 768 + 18 + 256 + 128 + 32 + 256

kernel_name = 'hybrid_nsa_mla_pool_trunk'


def rmsnorm(x, g):
    x32 = x.astype(jnp.float32)
    y = x32 * lax.rsqrt(jnp.mean(x32 * x32, axis=-1, keepdims=True) + EPS)
    return (y * g.astype(jnp.float32)).astype(x.dtype)


def masked_softmax(s, mask):
    s = jnp.where(mask, s.astype(jnp.float32), -jnp.inf)
    m = jnp.max(s, axis=-1, keepdims=True)
    m = jnp.where(jnp.isfinite(m), m, 0.0)
    e = jnp.exp(s - m)
    d = jnp.sum(e, axis=-1, keepdims=True)
    return e / jnp.maximum(d, jnp.finfo(jnp.float32).tiny)


def rope_cos_sin(positions, dim):
    half = dim // 2
    inv = jnp.power(jnp.float32(ROPE_THETA), -(jnp.arange(half, dtype=jnp.float32) / half))
    ang = positions.astype(jnp.float32)[..., None] * inv
    return jnp.cos(ang), jnp.sin(ang)


def apply_rope(x, cos, sin):
    half = x.shape[-1] // 2
    x1, x2 = x[..., :half], x[..., half:]
    cos = cos.astype(x.dtype)
    sin = sin.astype(x.dtype)
    return jnp.concatenate([x1 * cos - x2 * sin, x2 * cos + x1 * sin], axis=-1)


def nsa_compress(t, pos, w1, w2):
    B, G, T, DH = t.shape
    n_chunks = T // NSA_CMP_STRIDE
    r = NSA_CMP_LEN // NSA_CMP_STRIDE
    c = t.reshape(B, G, n_chunks, NSA_CMP_STRIDE, DH)
    blocks = jnp.concatenate([c[:, :, i:n_chunks - r + 1 + i] for i in range(r)], axis=3)
    h = jax.nn.gelu(jnp.einsum('bgnld,lde->bgne', blocks + pos, w1))
    return jnp.einsum('bgne,ef->bgnf', h, w2)


def cmp_to_slc_overlap(n_cmp, n_slc):
    start = np.arange(n_cmp)[:, None] * NSA_CMP_STRIDE
    end = start + NSA_CMP_LEN
    s0 = np.arange(n_slc)[None, :] * NSA_SLC_LEN
    s1 = s0 + NSA_SLC_LEN
    ov = np.clip(np.minimum(end, s1) - np.maximum(start, s0), 0, None) / NSA_CMP_LEN
    return jnp.asarray(ov, dtype=jnp.float32)


def nsa_mixer(q, kv, gate, cmp_pos, cmp_w1, cmp_w2):
    B, T, _ = q.shape
    G, HPG, DH = NSA_KV_HEADS, NSA_HPG, NSA_DH
    q = q.reshape(B, T, G, HPG, DH).transpose(0, 2, 3, 1, 4)
    kv = kv.reshape(B, T, 3, 2, G, DH).transpose(2, 3, 0, 4, 1, 5)
    g = jax.nn.sigmoid(gate).reshape(B, T, G, HPG, 3).transpose(0, 2, 3, 1, 4)

    k_cmp = nsa_compress(kv[0, 0], cmp_pos[0], cmp_w1[0], cmp_w2[0])
    v_cmp = nsa_compress(kv[0, 1], cmp_pos[1], cmp_w1[1], cmp_w2[1])
    n_cmp = k_cmp.shape[2]
    n_slc = T // NSA_SLC_LEN
    n_sel = min(NSA_TOPN, n_slc)
    overlap = cmp_to_slc_overlap(n_cmp, n_slc)
    k_blk = kv[1, 0].reshape(B, G, n_slc, NSA_SLC_LEN, DH)
    v_blk = kv[1, 1].reshape(B, G, n_slc, NSA_SLC_LEN, DH)
    pad = ((0, 0), (0, 0), (NSA_WINDOW, 0), (0, 0))
    k_win = jnp.pad(kv[2, 0], pad)
    v_win = jnp.pad(kv[2, 1], pad)

    cmp_end = jnp.arange(n_cmp) * NSA_CMP_STRIDE + NSA_CMP_LEN - 1
    blk_ids = jnp.arange(n_slc)
    scale = DH ** -0.5
    gather = jax.vmap(jax.vmap(lambda blocks, ix: blocks[ix]))

    def block_fn(c):
        q0 = c * NSA_Q_BLOCK
        qc = lax.dynamic_slice_in_dim(q, q0, NSA_Q_BLOCK, axis=3)
        gc = lax.dynamic_slice_in_dim(g, q0, NSA_Q_BLOCK, axis=3)
        qpos = q0 + jnp.arange(NSA_Q_BLOCK)
        s = jnp.einsum('bghqd,bgnd->bghqn', qc, k_cmp) * scale
        p_cmp = masked_softmax(s, cmp_end[None, :] <= qpos[:, None])
        o_cmp = jnp.einsum('bghqn,bgnd->bghqd', p_cmp.astype(v_cmp.dtype), v_cmp)
        imp = jnp.einsum('bghqn,ns->bgqs', p_cmp, overlap)
        cur = (qpos // NSA_SLC_LEN)[:, None]
        forced = (blk_ids[None, :] == 0) | (blk_ids[None, :] == cur) | (blk_ids[None, :] == cur - 1)
        valid = blk_ids[None, :] * NSA_SLC_LEN <= qpos[:, None]
        score = jnp.where(forced, NSA_FORCE, jnp.where(valid, imp, -1.0))
        _, idx = lax.top_k(score, n_sel)
        ks = gather(k_blk, idx).reshape(B, G, NSA_Q_BLOCK, n_sel * NSA_SLC_LEN, DH)
        vs = gather(v_blk, idx).reshape(B, G, NSA_Q_BLOCK, n_sel * NSA_SLC_LEN, DH)
        kpos = (idx[..., None] * NSA_SLC_LEN + jnp.arange(NSA_SLC_LEN)).reshape(B, G, NSA_Q_BLOCK, n_sel * NSA_SLC_LEN)
        s = jnp.einsum('bghqd,bgqkd->bghqk', qc, ks) * scale
        p = masked_softmax(s, (kpos <= qpos[:, None])[:, :, None])
        o_slc = jnp.einsum('bghqk,bgqkd->bghqd', p.astype(vs.dtype), vs)
        kw = lax.dynamic_slice_in_dim(k_win, q0, NSA_Q_BLOCK + NSA_WINDOW, axis=2)
        vw = lax.dynamic_slice_in_dim(v_win, q0, NSA_Q_BLOCK + NSA_WINDOW, axis=2)
        wpos = q0 - NSA_WINDOW + jnp.arange(NSA_Q_BLOCK + NSA_WINDOW)
        dist = qpos[:, None] - wpos[None, :]
        mask_w = (dist >= 0) & (dist < NSA_WINDOW) & (wpos[None, :] >= 0)
        s = jnp.einsum('bghqd,bgkd->bghqk', qc, kw) * scale
        p = masked_softmax(s, mask_w)
        o_win = jnp.einsum('bghqk,bgkd->bghqd', p.astype(vw.dtype), vw)
        return gc[..., 0:1] * o_cmp + gc[..., 1:2] * o_slc + gc[..., 2:3] * o_win

    o = lax.map(block_fn, jnp.arange(T // NSA_Q_BLOCK))
    return o.transpose(1, 0, 4, 2, 3, 5).reshape(B, T, NSA_HEADS * DH)


def mla_mixer(cq, ckv, k_rope, positions, q_norm_g, w_qup, kv_norm_g, w_kvup):
    B, T, _ = cq.shape
    q = (rmsnorm(cq, q_norm_g) @ w_qup).reshape(B, T, MLA_HEADS, MLA_NOPE + MLA_ROPE)
    kv = (rmsnorm(ckv, kv_norm_g) @ w_kvup).reshape(B, T, MLA_HEADS, MLA_NOPE + MLA_VDIM)
    cos, sin = rope_cos_sin(positions, MLA_ROPE)
    q_rope = apply_rope(q[..., MLA_NOPE:], cos[:, :, None], sin[:, :, None])
    k_rope = apply_rope(k_rope, cos, sin)
    q = jnp.concatenate([q[..., :MLA_NOPE], q_rope], axis=-1).transpose(0, 2, 1, 3)
    k = jnp.concatenate([kv[..., :MLA_NOPE],
                         jnp.broadcast_to(k_rope[:, :, None, :], (B, T, MLA_HEADS, MLA_ROPE))],
                        axis=-1).transpose(0, 2, 1, 3)
    v = kv[..., MLA_NOPE:].transpose(0, 2, 1, 3)
    scale = (MLA_NOPE + MLA_ROPE) ** -0.5
    kpos = jnp.arange(T)

    def block_fn(c):
        q0 = c * MLA_Q_BLOCK
        qc = lax.dynamic_slice_in_dim(q, q0, MLA_Q_BLOCK, axis=2)
        qpos = q0 + jnp.arange(MLA_Q_BLOCK)
        s = jnp.einsum('bhqd,bhkd->bhqk', qc, k).astype(jnp.float32) * scale
        s = jnp.where(kpos[None, :] <= qpos[:, None], s, -jnp.inf)
        p = jax.nn.softmax(s, axis=-1)
        return jnp.einsum('bhqk,bhkd->bhqd', p.astype(v.dtype), v)

    o = lax.map(block_fn, jnp.arange(T // MLA_Q_BLOCK))
    return o.transpose(1, 0, 3, 2, 4).reshape(B, T, MLA_HEADS * MLA_VDIM)


def pool_mixer(u, w_pool, scale):
    B, T, _ = u.shape
    u32 = u.astype(jnp.float32)
    csum = jnp.pad(jnp.cumsum(u32, axis=1), ((0, 0), (1, 0), (0, 0)))
    t_idx = jnp.arange(T)
    means = []
    for gi, w in enumerate(POOL_WINDOWS):
        cg = csum[..., gi * POOL_GROUP:(gi + 1) * POOL_GROUP]
        upper = cg[:, 1:T + 1]
        lower = jnp.pad(cg, ((0, 0), (w, 0), (0, 0)))[:, 1:T + 1]
        cnt = jnp.minimum(t_idx + 1, w).astype(jnp.float32)[None, :, None]
        means.append((upper - lower) / cnt)
    pooled = (jnp.concatenate(means, axis=-1) - u32).astype(u.dtype)
    pooled = pooled.reshape(B, T, len(POOL_WINDOWS), POOL_GROUP)
    y = jnp.einsum('btgc,gcd->btgd', pooled, w_pool).reshape(B, T, POOL_WIDTH)
    return y * scale


def setup_inputs(seed: int = 0) -> dict:
    key = jax.random.key(seed)
    ks = jax.random.split(key, 20)
    f32 = jnp.float32

    def nrm(k, shape, s):
        return jax.random.normal(k, shape, f32) * s

    x = jax.random.normal(ks[0], (BATCH, SEQ, D_MODEL), f32)
    positions = (jnp.arange(SEQ, dtype=jnp.int32)[None, :]
                 + jax.random.randint(ks[1], (BATCH, 1), 0, 1024, dtype=jnp.int32))
    return {
        'x': x,
        'positions': positions,
        'ln1_g': 1.0 + nrm(ks[2], (DEPTH, D_MODEL), 0.05),
        'w_in': nrm(ks[3], (DEPTH, D_MODEL, P_IN), D_MODEL ** -0.5),
        'nsa_cmp_pos': nrm(ks[4], (DEPTH, 2, NSA_CMP_LEN, NSA_DH), 0.1),
        'nsa_cmp_w1': nrm(ks[5], (DEPTH, 2, NSA_CMP_LEN, NSA_DH, NSA_DH), (NSA_CMP_LEN * NSA_DH) ** -0.5),
        'nsa_cmp_w2': nrm(ks[6], (DEPTH, 2, NSA_DH, NSA_DH), NSA_DH ** -0.5),
        'mla_q_norm': 1.0 + nrm(ks[7], (DEPTH, MLA_Q_LORA), 0.05),
        'mla_w_qup': nrm(ks[8], (DEPTH, MLA_Q_LORA, MLA_HEADS * (MLA_NOPE + MLA_ROPE)), MLA_Q_LORA ** -0.5),
        'mla_kv_norm': 1.0 + nrm(ks[9], (DEPTH, MLA_KV_LORA), 0.05),
        'mla_w_kvup': nrm(ks[10], (DEPTH, MLA_KV_LORA, MLA_HEADS * (MLA_NOPE + MLA_VDIM)), MLA_KV_LORA ** -0.5),
        'pool_w': nrm(ks[11], (DEPTH, len(POOL_WINDOWS), POOL_GROUP, POOL_GROUP), POOL_GROUP ** -0.5),
        'pool_scale': 1.0 + nrm(ks[12], (DEPTH, POOL_WIDTH), 0.1),
        'w_out': nrm(ks[13], (DEPTH, D_MIX, D_MODEL), D_MIX ** -0.5),
        'ln2_g': 1.0 + nrm(ks[14], (DEPTH, D_MODEL), 0.05),
        'w_ff1': nrm(ks[15], (DEPTH, D_MODEL, D_FF), D_MODEL ** -0.5),
        'w_ff2': nrm(ks[16], (DEPTH, D_FF, D_MODEL), D_FF ** -0.5),
        'final_g': 1.0 + nrm(ks[17], (D_MODEL,), 0.05),
    }


def reference(x, positions, ln1_g, w_in, nsa_cmp_pos, nsa_cmp_w1, nsa_cmp_w2, mla_q_norm, mla_w_qup,
              mla_kv_norm, mla_w_kvup, pool_w, pool_scale, w_out, ln2_g, w_ff1, w_ff2, final_g):
    offsets = np.cumsum(IN_SPLITS)[:-1].tolist()
    for l in range(DEPTH):
        h = rmsnorm(x, ln1_g[l])
        z = h @ w_in[l]
        nq, nkv, ngate, cq, ckv, kr, pu = jnp.split(z, offsets, axis=-1)
        o_nsa = nsa_mixer(nq, nkv, ngate, nsa_cmp_pos[l], nsa_cmp_w1[l], nsa_cmp_w2[l])
        o_mla = mla_mixer(cq, ckv, kr, positions, mla_q_norm[l], mla_w_qup[l], mla_kv_norm[l], mla_w_kvup[l])
        o_pool = pool_mixer(pu, pool_w[l], pool_scale[l])
        mix = jnp.concatenate([o_nsa, o_mla, o_pool], axis=-1)
        x = x + mix @ w_out[l]
        h = rmsnorm(x, ln2_g[l])
        x = x + jnp.square(jax.nn.relu(h @ w_ff1[l])) @ w_ff2[l]
    return rmsnorm(x, final_g)
```

```python
import functools

import numpy as np
import jax
import jax.numpy as jnp
from jax import lax
from jax.experimental import pallas as pl
from jax.experimental.pallas import tpu as pltpu

F32 = jnp.float32
BF16 = jnp.bfloat16

D_MODEL = 1024
NSA_HEADS = 6
NSA_KV_HEADS = 2
NSA_HPG = 3
NSA_DH = 64
NSA_CMP_LEN = 32
NSA_CMP_STRIDE = 16
NSA_SLC_LEN = 64
NSA_TOPN = 16
NSA_WINDOW = 512
NSA_FORCE = 1.0e4
MLA_HEADS = 6
MLA_Q_LORA = 256
MLA_KV_LORA = 128
MLA_NOPE = 64
MLA_ROPE = 32
MLA_VDIM = 64
ROPE_THETA = 10000.0
POOL_WINDOWS = (2, 4, 8, 16)
POOL_GROUP = 64
POOL_WIDTH = 256
POOL_HALO = 16
D_FF = 4096
EPS = 1e-6
P_IN = 1842

LANES = 128
VMEM_LIMIT = 56 * 1024 * 1024

MASKED = -1.0e30
M_FLOOR = -1.0e20

ROW_TILE = 512
ATT_TILE = 256
FF_CHUNK = 1024


def _dot(a, b):
    return jnp.dot(a, b, preferred_element_type=F32)


def _dot_nt(a, b):
    return lax.dot_general(a, b, (((1,), (1,)), ((), ())), preferred_element_type=F32)


def _rms(x, g):
    return x * lax.rsqrt(jnp.mean(x * x, axis=-1, keepdims=True) + EPS) * g


def _layout(spec):
    idx, scl = [], []
    for start, n, s in spec:
        if start is None:
            idx.extend([0] * n)
            scl.extend([0.0] * n)
        else:
            idx.extend(range(start, start + n))
            scl.extend([s] * n)
    return np.asarray(idx, np.int32), np.asarray(scl, np.float32)


def _in_layout():
    spec = []
    for h in range(NSA_HEADS):
        spec += [(h * 64, 64, 0.125), (None, 64, 0.0)]
    for br in range(3):
        for g in range(2):
            for kvi in range(2):
                spec += [(384 + br * 256 + kvi * 128 + g * 64, 64, 1.0)]
    for g in range(2):
        spec += [(1152 + g * 9, 9, 1.0), (None, LANES - 9, 0.0)]
    spec += [(1170, 256, 1.0), (1426, 128, 1.0)]
    spec += [(None, 64, 0.0), (1554, 32, 1.0), (None, 32, 0.0)]
    spec += [(None, 64, 0.0), (1570, 16, -1.0), (1554, 16, 1.0), (None, 32, 0.0)]
    spec += [(1586, 256, 1.0)]
    return _layout(spec)


def _mla_layouts():
    q, qr, k, v = [], [], [], []
    for h in range(MLA_HEADS):
        b = h * 96
        q += [(b, 96, 1.0), (None, 32, 0.0)]
        qr += [(None, 64, 0.0), (b + 80, 16, -1.0), (b + 64, 16, 1.0), (None, 32, 0.0)]
        k += [(h * 128, 64, 1.0), (None, 64, 0.0)]
        v += [(h * 128 + 64, 64, 1.0)]
    return _layout(q), _layout(qr), _layout(k), _layout(v)


def _take_cols(w, layout):
    idx, scl = layout
    return (jnp.take(w, idx, axis=1) * scl[None, :]).astype(BF16)


def _take_rows(w, layout):
    idx, scl = layout
    return (jnp.take(w, idx, axis=0) * scl[:, None]).astype(BF16)


def _blockdiag2(a, b):
    z = jnp.zeros_like(a)
    top = jnp.concatenate([a, z], axis=-1)
    bot = jnp.concatenate([z, b], axis=-1)
    return jnp.concatenate([top, bot], axis=-2)


def _const_spec(shape):
    nd = len(shape)
    return pl.BlockSpec(shape, lambda *_: (0,) * nd, pipeline_mode=pl.Buffered(1))


def _params(sem):
    return pltpu.CompilerParams(dimension_semantics=sem, vmem_limit_bytes=VMEM_LIMIT)


def _rope_kernel(pos_ref, inv_ref, cos_ref, sin_ref):
    ang = pos_ref[...].astype(F32) * inv_ref[...]
    cos_ref[...] = jnp.cos(ang)
    sin_ref[...] = jnp.sin(ang)


def _rope_tables(positions):
    n = positions.size
    half = MLA_ROPE // 2
    inv = jnp.power(jnp.float32(ROPE_THETA), -(jnp.arange(half, dtype=F32) / half))
    inv128 = jnp.zeros((1, LANES), F32).at[0, 64:64 + half].set(inv).at[0, 64 + half:64 + 2 * half].set(inv)
    tm = ROW_TILE
    out = jax.ShapeDtypeStruct((n, LANES), F32)
    return pl.pallas_call(
        _rope_kernel,
        grid=(n // tm,),
        in_specs=[pl.BlockSpec((tm, 1), lambda i: (i, 0)), _const_spec((1, LANES))],
        out_specs=[pl.BlockSpec((tm, LANES), lambda i: (i, 0))] * 2,
        out_shape=[out, out],
        compiler_params=_params(("parallel",)),
        name="rope_tables",
    )(positions.reshape(n, 1), inv128)


def _proj_in_kernel(x_ref, g_ref, w_ref, cos_ref, sin_ref, qg_ref, kvg_ref, wq_ref, wqr_ref, wk_ref, wv_ref,
                    q6_ref, kvc_ref, kvs_ref, kvw_ref, gate_ref, mq_ref, mk_ref, mv_ref, u_ref):
    h = _rms(x_ref[...], g_ref[...]).astype(BF16)
    q6_ref[...] = _dot(h, w_ref[:, 0:768]).astype(BF16)
    z = _dot(h, w_ref[:, 768:1536])
    kvc_ref[...] = z[:, 0:256]
    kvs_ref[...] = z[:, 256:512].astype(BF16)
    kvw_ref[...] = z[:, 512:768].astype(BF16)
    z = _dot(h, w_ref[:, 1536:2688])
    gate_ref[...] = jax.nn.sigmoid(z[:, 0:256])
    u_ref[...] = z[:, 896:1152]
    cos = cos_ref[...]
    sin = sin_ref[...]
    cqn = _rms(z[:, 256:512], qg_ref[...]).astype(BF16)
    qa = _dot(cqn, wq_ref[...])
    qb = _dot(cqn, wqr_ref[...])
    scale = (MLA_NOPE + MLA_ROPE) ** -0.5
    for hh in range(MLA_HEADS):
        sl = slice(hh * LANES, (hh + 1) * LANES)
        mq_ref[:, sl] = ((qa[:, sl] * cos + qb[:, sl] * sin) * scale).astype(BF16)
    ckvn = _rms(z[:, 512:640], kvg_ref[...]).astype(BF16)
    k_rope = z[:, 640:768] * cos + z[:, 768:896] * sin
    ka = _dot(ckvn, wk_ref[...])
    for hh in range(MLA_HEADS):
        sl = slice(hh * LANES, (hh + 1) * LANES)
        mk_ref[:, sl] = (ka[:, sl] + k_rope).astype(BF16)
    mv_ref[...] = _dot(ckvn, wv_ref[...]).astype(BF16)


def _proj_in(x, g, w_all, cos, sin, qg, kvg, wq, wqr, wk, wv):
    n = x.shape[0]
    tm = ROW_TILE
    row = lambda width: pl.BlockSpec((tm, width), lambda i: (i, 0))
    outs = [(768, BF16), (256, F32), (256, BF16), (256, BF16), (256, F32), (768, BF16), (768, BF16), (384, BF16),
            (256, F32)]
    return pl.pallas_call(
        _proj_in_kernel,
        grid=(n // tm,),
        in_specs=[row(D_MODEL), _const_spec(g.shape), _const_spec(w_all.shape), row(LANES), row(LANES),
                  _const_spec(qg.shape), _const_spec(kvg.shape), _const_spec(wq.shape), _const_spec(wqr.shape),
                  _const_spec(wk.shape), _const_spec(wv.shape)],
        out_specs=[row(w) for w, _ in outs],
        out_shape=[jax.ShapeDtypeStruct((n, w), dt) for w, dt in outs],
        compiler_params=_params(("parallel",)),
        name="proj_in",
    )(x, g, w_all, cos, sin, qg, kvg, wq, wqr, wk, wv)


def _compress_kernel(kvc_ref, posa_ref, posb_ref, wa_ref, wb_ref, w2_ref, o_ref, *, ncp):
    a = jnp.zeros((ncp, LANES), F32)
    b = jnp.zeros((ncp, LANES), F32)
    for l in range(NSA_CMP_STRIDE):
        xl = kvc_ref[pl.ds(l, ncp, stride=NSA_CMP_STRIDE), :]
        a = a + _dot((xl + posa_ref[l]).astype(BF16), wa_ref[l])
        b = b + _dot((xl + posb_ref[l]).astype(BF16), wb_ref[l])
    pre = a + pltpu.roll(b, ncp - 1, axis=0)
    hid = jax.nn.gelu(pre).astype(BF16)
    o_ref[...] = _dot(hid, w2_ref[...]).astype(BF16)


def _compress(kvc, posa, posb, wa, wb, w2, batch, seq):
    ncp = seq // NSA_CMP_STRIDE
    return pl.pallas_call(
        functools.partial(_compress_kernel, ncp=ncp),
        grid=(batch, NSA_KV_HEADS),
        in_specs=[pl.BlockSpec((seq, LANES), lambda b, g: (b, g)), _const_spec(posa.shape), _const_spec(posb.shape),
                  _const_spec(wa.shape), _const_spec(wb.shape), _const_spec(w2.shape)],
        out_specs=pl.BlockSpec((ncp, LANES), lambda b, g: (b, g)),
        out_shape=jax.ShapeDtypeStruct((batch * ncp, 2 * LANES), BF16),
        compiler_params=_params(("parallel", "parallel")),
        name="nsa_compress",
    )(kvc, posa, posb, wa, wb, w2)


def _flash_init(m_sc, l_sc, acc_sc):
    m_sc[...] = jnp.full(m_sc.shape, M_FLOOR, F32)
    l_sc[...] = jnp.zeros(l_sc.shape, F32)
    acc_sc[...] = jnp.zeros(acc_sc.shape, F32)


def _flash_update(s, v, m_sc, l_sc, acc_sc):
    m_old = m_sc[...]
    m_new = jnp.maximum(m_old, jnp.max(s, axis=-1, keepdims=True))
    alpha = jnp.exp(m_old - m_new)
    p = jnp.exp(s - m_new)
    l_sc[...] = alpha * l_sc[...] + jnp.sum(p, axis=-1, keepdims=True)
    acc_sc[...] = alpha * acc_sc[...] + _dot(p.astype(BF16), v)
    m_sc[...] = m_new


def _flash_result(l_sc, acc_sc):
    return acc_sc[...] * (1.0 / l_sc[...])


def _nsa_kernel(q_ref, kvc_ref, kvs_ref, kvw_ref, gate_ref, ov_ref, et_ref, o_ref, m_sc, l_sc, acc_sc,
                *, tq, n_slc, n_sel):
    i = pl.program_id(2)
    q0 = pl.multiple_of(i * tq, tq)
    rows = NSA_HPG * tq
    q3 = jnp.concatenate([q_ref[:, h * LANES:(h + 1) * LANES] for h in range(NSA_HPG)], axis=0)

    kvc = kvc_ref[...]
    ncp = kvc.shape[0]
    s = _dot_nt(q3, kvc)
    r = lax.broadcasted_iota(jnp.int32, (rows, ncp), 0) & (tq - 1)
    n = lax.broadcasted_iota(jnp.int32, (rows, ncp), 1)
    s = jnp.where(n * NSA_CMP_STRIDE + (NSA_CMP_LEN - 1) <= q0 + r, s, -jnp.inf)
    m = jnp.max(s, axis=-1, keepdims=True)
    m = jnp.where(m == -jnp.inf, 0.0, m)
    e = jnp.exp(s - m)
    d = jnp.sum(e, axis=-1, keepdims=True)
    p = e * (1.0 / jnp.where(d > 0.0, d, 1.0))
    o_cmp = _dot(p.astype(BF16), kvc)

    psum = p[0:tq] + p[tq:2 * tq] + p[2 * tq:3 * tq]
    hi = psum.astype(BF16)
    lo = (psum - hi.astype(F32)).astype(BF16)
    ov = ov_ref[...]
    imp = _dot(hi, ov) + _dot(lo, ov)
    blk = lax.broadcasted_iota(jnp.int32, (tq, LANES), 1)
    qp = q0 + lax.broadcasted_iota(jnp.int32, (tq, LANES), 0)
    cur = qp >> 6
    forced = (blk == 0) | (blk == cur) | (blk == cur - 1)
    valid = (blk << 6) <= qp
    score = jnp.where(forced, NSA_FORCE, jnp.where(valid, imp, -1.0))
    score = jnp.where(blk < n_slc, score, -2.0)
    sc_t = score.T[0:n_slc]
    sub = 8
    sidx = lax.broadcasted_iota(jnp.int32, (sub, tq), 0)
    groups = [sc_t[k * sub:(k + 1) * sub] for k in range(n_slc // sub)]
    cnts = [jnp.zeros((sub, tq), F32) for _ in groups]
    for sp in range(n_slc):
        col = sc_t[sp:sp + 1, :]
        for k, grp in enumerate(groups):
            if k * sub > sp:
                cnts[k] = cnts[k] + jnp.where(col >= grp, 1.0, 0.0)
            elif (k + 1) * sub - 1 <= sp:
                cnts[k] = cnts[k] + jnp.where(col > grp, 1.0, 0.0)
            else:
                after = jnp.where(col >= grp, 1.0, 0.0)
                cnts[k] = cnts[k] + jnp.where(sidx + k * sub > sp, after, jnp.where(col > grp, 1.0, 0.0))
    cnt = jnp.concatenate(cnts, axis=0)
    bias_t = jnp.where(cnt < n_sel, 0.0, MASKED)
    bias_t = jnp.concatenate([bias_t, jnp.zeros((LANES - n_slc, tq), F32)], axis=0)
    bias = bias_t.T.astype(BF16)
    bias3 = jnp.concatenate([bias] * NSA_HPG, axis=0)

    rr = lax.broadcasted_iota(jnp.int32, (rows, tq), 0) & (tq - 1)
    cc = lax.broadcasted_iota(jnp.int32, (rows, tq), 1)

    def slc_scores(off):
        kv = kvs_ref[pl.ds(off, tq), :]
        s = _dot_nt(q3, kv) + _dot(bias3, et_ref[:, pl.ds(off, tq)])
        return s, kv

    _flash_init(m_sc, l_sc, acc_sc)

    def slc_body(j, carry):
        s, kv = slc_scores(pl.multiple_of(j * tq, tq))
        _flash_update(s, kv, m_sc, l_sc, acc_sc)
        return carry

    lax.fori_loop(0, i, slc_body, 0)
    s, kv = slc_scores(q0)
    _flash_update(jnp.where(cc <= rr, s, MASKED), kv, m_sc, l_sc, acc_sc)
    o_slc = _flash_result(l_sc, acc_sc)

    _flash_init(m_sc, l_sc, acc_sc)
    back = NSA_WINDOW // tq

    @pl.when(i >= back)
    def _():
        kv = kvw_ref[pl.ds(pl.multiple_of(q0 - back * tq, tq), tq), :]
        _flash_update(jnp.where(cc > rr, _dot_nt(q3, kv), MASKED), kv, m_sc, l_sc, acc_sc)

    for b in range(back - 1, 0, -1):
        @pl.when(i >= b)
        def _():
            kv = kvw_ref[pl.ds(pl.multiple_of(q0 - b * tq, tq), tq), :]
            _flash_update(_dot_nt(q3, kv), kv, m_sc, l_sc, acc_sc)

    kv = kvw_ref[pl.ds(q0, tq), :]
    _flash_update(jnp.where(cc <= rr, _dot_nt(q3, kv), MASKED), kv, m_sc, l_sc, acc_sc)
    o_win = _flash_result(l_sc, acc_sc)

    gt = gate_ref[...]
    lane = lax.broadcasted_iota(jnp.int32, (tq, LANES), 1)
    for h in range(NSA_HPG):
        sl = slice(h * tq, (h + 1) * tq)
        o = (gt[:, 3 * h:3 * h + 1] * o_cmp[sl] + gt[:, 3 * h + 1:3 * h + 2] * o_slc[sl]
             + gt[:, 3 * h + 2:3 * h + 3] * o_win[sl])
        o_ref[:, h * LANES:(h + 1) * LANES] = jnp.where(lane >= NSA_DH, o, 0.0).astype(BF16)


def _nsa_attention(q6, kvcmp, kvs, kvw, gate, ov, et, batch, seq):
    tq = ATT_TILE
    nq = seq // tq
    ncp = seq // NSA_CMP_STRIDE
    n_slc = seq // NSA_SLC_LEN
    rows = NSA_HPG * tq
    gw = NSA_HPG * LANES
    kern = functools.partial(_nsa_kernel, tq=tq, n_slc=n_slc, n_sel=min(NSA_TOPN, n_slc))
    return pl.pallas_call(
        kern,
        grid=(batch, NSA_KV_HEADS, nq),
        in_specs=[pl.BlockSpec((tq, gw), lambda b, g, i: (b * nq + i, g)),
                  pl.BlockSpec((ncp, LANES), lambda b, g, i: (b, g)),
                  pl.BlockSpec((seq, LANES), lambda b, g, i: (b, g)),
                  pl.BlockSpec((seq, LANES), lambda b, g, i: (b, g)),
                  pl.BlockSpec((tq, LANES), lambda b, g, i: (b * nq + i, g)),
                  _const_spec(ov.shape), _const_spec(et.shape)],
        out_specs=pl.BlockSpec((tq, gw), lambda b, g, i: (b * nq + i, g)),
        out_shape=jax.ShapeDtypeStruct((batch * seq, NSA_KV_HEADS * gw), BF16),
        scratch_shapes=[pltpu.VMEM((rows, 1), F32), pltpu.VMEM((rows, 1), F32), pltpu.VMEM((rows, LANES), F32)],
        compiler_params=_params(("parallel", "parallel", "arbitrary")),
        name="nsa_attention",
    )(q6, kvcmp, kvs, kvw, gate, ov, et)


def _mla_kernel(q_ref, k_ref, v_ref, o_ref, m_sc, l_sc, acc_sc, *, tq):
    i = pl.program_id(2)
    q0 = pl.multiple_of(i * tq, tq)
    rr = lax.broadcasted_iota(jnp.int32, (tq, tq), 0)
    cc = lax.broadcasted_iota(jnp.int32, (tq, tq), 1)
    outs = []
    for hh in range(2):
        sl = slice(hh * LANES, (hh + 1) * LANES)
        q = q_ref[:, sl]
        _flash_init(m_sc, l_sc, acc_sc)

        def body(j, carry):
            off = pl.multiple_of(j * tq, tq)
            _flash_update(_dot_nt(q, k_ref[pl.ds(off, tq), sl]), v_ref[pl.ds(off, tq), :], m_sc, l_sc, acc_sc)
            return carry

        lax.fori_loop(0, i, body, 0)
        s = _dot_nt(q, k_ref[pl.ds(q0, tq), sl])
        _flash_update(jnp.where(cc <= rr, s, MASKED), v_ref[pl.ds(q0, tq), :], m_sc, l_sc, acc_sc)
        outs.append(_flash_result(l_sc, acc_sc))
    lane = lax.broadcasted_iota(jnp.int32, (tq, LANES), 1)
    o_ref[...] = jnp.where(lane < MLA_VDIM, outs[0], outs[1]).astype(BF16)


def _mla_attention(mq, mk, mv, batch, seq):
    tq = ATT_TILE
    nq = seq // tq
    pairs = MLA_HEADS // 2
    return pl.pallas_call(
        functools.partial(_mla_kernel, tq=tq),
        grid=(batch, pairs, nq),
        in_specs=[pl.BlockSpec((tq, 2 * LANES), lambda b, p, i: (b * nq + i, p)),
                  pl.BlockSpec((seq, 2 * LANES), lambda b, p, i: (b, p)),
                  pl.BlockSpec((seq, LANES), lambda b, p, i: (b, p))],
        out_specs=pl.BlockSpec((tq, LANES), lambda b, p, i: (b * nq + i, p)),
        out_shape=jax.ShapeDtypeStruct((batch * seq, pairs * LANES), BF16),
        scratch_shapes=[pltpu.VMEM((tq, 1), F32), pltpu.VMEM((tq, 1), F32), pltpu.VMEM((tq, LANES), F32)],
        compiler_params=_params(("parallel", "parallel", "arbitrary")),
        name="mla_attention",
    )(mq, mk, mv)


def _out_ffn_kernel(x_ref, on_ref, om_ref, u_ref, uh_ref, pw_ref, ps_ref, won_ref, wom_ref, wop_ref, g2_ref,
                    w1_ref, w2_ref, gf_ref, y_ref, *, tm, seq, final):
    i = pl.program_id(0)
    t0 = (i * tm) % seq
    u = u_ref[...]
    halo = jnp.where(t0 == 0, 0.0, uh_ref[...])
    ext = jnp.concatenate([halo, u], axis=0)
    s2 = ext + pltpu.roll(ext, 1, axis=0)
    s4 = s2 + pltpu.roll(s2, 2, axis=0)
    s8 = s4 + pltpu.roll(s4, 4, axis=0)
    s16 = s8 + pltpu.roll(s8, 8, axis=0)
    sl = slice(POOL_HALO, POOL_HALO + tm)
    lane = lax.broadcasted_iota(jnp.int32, (tm, POOL_WIDTH), 1)
    t = t0 + lax.broadcasted_iota(jnp.int32, (tm, POOL_WIDTH), 0)
    sums = jnp.where(lane < 64, s2[sl], jnp.where(lane < 128, s4[sl], jnp.where(lane < 192, s8[sl], s16[sl])))
    win = jnp.where(lane < 64, 2, jnp.where(lane < 128, 4, jnp.where(lane < 192, 8, 16)))
    cnt = jnp.minimum(t + 1, win).astype(F32)
    pooled = (sums / cnt - u).astype(BF16)
    y_pool = (_dot(pooled, pw_ref[...]) * ps_ref[...]).astype(BF16)
    mix = _dot(on_ref[...], won_ref[...]) + _dot(om_ref[...], wom_ref[...]) + _dot(y_pool, wop_ref[...])
    x = x_ref[...] + mix
    h = _rms(x, g2_ref[...]).astype(BF16)
    ff = None
    for c in range(D_FF // FF_CHUNK):
        cs = slice(c * FF_CHUNK, (c + 1) * FF_CHUNK)
        a = jnp.maximum(_dot(h, w1_ref[:, cs]), 0.0)
        part = _dot((a * a).astype(BF16), w2_ref[cs, :])
        ff = part if ff is None else ff + part
    acc = x + ff
    if final:
        acc = _rms(acc, gf_ref[...])
    y_ref[...] = acc


def _out_ffn(x, o_nsa, o_mla, u, pw, ps, won, wom, wop, g2, w1, w2, gf, seq, final):
    n = x.shape[0]
    tm = ROW_TILE
    hb = tm // POOL_HALO
    row = lambda width: pl.BlockSpec((tm, width), lambda i: (i, 0))
    return pl.pallas_call(
        functools.partial(_out_ffn_kernel, tm=tm, seq=seq, final=final),
        grid=(n // tm,),
        in_specs=[row(D_MODEL), row(o_nsa.shape[1]), row(o_mla.shape[1]), row(POOL_WIDTH),
                  pl.BlockSpec((POOL_HALO, POOL_WIDTH), lambda i: (jnp.maximum(i * hb - 1, 0), 0)),
                  _const_spec(pw.shape), _const_spec(ps.shape), _const_spec(won.shape), _const_spec(wom.shape),
                  _const_spec(wop.shape), _const_spec(g2.shape), _const_spec(w1.shape), _const_spec(w2.shape),
                  _const_spec(gf.shape)],
        out_specs=row(D_MODEL),
        out_shape=jax.ShapeDtypeStruct((n, D_MODEL), F32),
        compiler_params=_params(("parallel",)),
        name="out_ffn",
    )(x, o_nsa, o_mla, u, u, pw, ps, won, wom, wop, g2, w1, w2, gf)


def _overlap_padded(seq):
    ncp = seq // NSA_CMP_STRIDE
    n_slc = seq // NSA_SLC_LEN
    start = np.arange(ncp - 1)[:, None] * NSA_CMP_STRIDE
    end = start + NSA_CMP_LEN
    s0 = np.arange(n_slc)[None, :] * NSA_SLC_LEN
    s1 = s0 + NSA_SLC_LEN
    ov = np.clip(np.minimum(end, s1) - np.maximum(start, s0), 0, None) / NSA_CMP_LEN
    out = np.zeros((ncp, LANES), np.float32)
    out[:ncp - 1, :n_slc] = ov
    return jnp.asarray(out, BF16)


def _block_indicator(seq):
    et = (np.arange(LANES)[:, None] == (np.arange(seq)[None, :] // NSA_SLC_LEN)).astype(np.float32)
    return jnp.asarray(et, BF16)


def kernel(x, positions, ln1_g, w_in, nsa_cmp_pos, nsa_cmp_w1, nsa_cmp_w2, mla_q_norm, mla_w_qup, mla_kv_norm,
           mla_w_kvup, pool_w, pool_scale, w_out, ln2_g, w_ff1, w_ff2, final_g):
    batch, seq, _ = x.shape
    depth = w_in.shape[0]
    n = batch * seq
    assert n % ROW_TILE == 0 and seq % ROW_TILE == 0 and seq % ATT_TILE == 0
    assert seq // NSA_SLC_LEN <= LANES and NSA_WINDOW % ATT_TILE == 0

    cos, sin = _rope_tables(positions)
    ov = _overlap_padded(seq)
    et = _block_indicator(seq)
    in_layout = _in_layout()
    lq, lqr, lk, lv = _mla_layouts()
    won_layout = _layout([r for h in range(NSA_HEADS) for r in ((None, 64, 0.0), (h * 64, 64, 1.0))])
    row2 = lambda v: v.reshape(1, -1)

    xf = x.reshape(n, D_MODEL)
    for l in range(depth):
        w_all = _take_cols(w_in[l], in_layout)
        wq, wqr = _take_cols(mla_w_qup[l], lq), _take_cols(mla_w_qup[l], lqr)
        wk, wv = _take_cols(mla_w_kvup[l], lk), _take_cols(mla_w_kvup[l], lv)
        q6, kvc, kvs, kvw, gate, mq, mk, mv, u = _proj_in(
            xf, row2(ln1_g[l]), w_all, cos, sin, row2(mla_q_norm[l]), row2(mla_kv_norm[l]), wq, wqr, wk, wv)

        w1c = _blockdiag2(nsa_cmp_w1[l, 0], nsa_cmp_w1[l, 1]).astype(BF16)
        w2c = _blockdiag2(nsa_cmp_w2[l, 0], nsa_cmp_w2[l, 1]).astype(BF16)
        pos = jnp.concatenate([nsa_cmp_pos[l, 0], nsa_cmp_pos[l, 1]], axis=-1)[:, None, :]
        half = NSA_CMP_STRIDE
        kvcmp = _compress(kvc, pos[:half], pos[half:], w1c[:half], w1c[half:], w2c, batch, seq)

        o_nsa = _nsa_attention(q6, kvcmp, kvs, kvw, gate, ov, et, batch, seq)
        o_mla = _mla_attention(mq, mk, mv, batch, seq)

        pw = _blockdiag2(_blockdiag2(pool_w[l, 0], pool_w[l, 1]), _blockdiag2(pool_w[l, 2], pool_w[l, 3])).astype(BF16)
        won = _take_rows(w_out[l], won_layout)
        wom = w_out[l, 384:768].astype(BF16)
        wop = w_out[l, 768:1024].astype(BF16)
        xf = _out_ffn(xf, o_nsa, o_mla, u, pw, row2(pool_scale[l]), won, wom, wop, row2(ln2_g[l]),
                      w_ff1[l].astype(BF16), w_ff2[l].astype(BF16), row2(final_g), seq, final=(l == depth - 1))
    return xf.reshape(batch, seq, D_MODEL)
```

```python
import functools

import numpy as np
import jax
import jax.numpy as jnp
from jax import lax
from jax.experimental import pallas as pl
from jax.experimental.pallas import tpu as pltpu

F32 = jnp.float32
BF16 = jnp.bfloat16

D_MODEL = 1024
NSA_HEADS = 6
NSA_KV_HEADS = 2
NSA_HPG = 3
NSA_DH = 64
NSA_CMP_LEN = 32
NSA_CMP_STRIDE = 16
NSA_SLC_LEN = 64
NSA_TOPN = 16
NSA_WINDOW = 512
NSA_FORCE = 1.0e4
MLA_HEADS = 6
MLA_NOPE = 64
MLA_ROPE = 32
MLA_VDIM = 64
ROPE_THETA = 10000.0
POOL_WIDTH = 256
POOL_HALO = 16
D_FF = 4096
EPS = 1e-6

LANES = 128
SUBLANES = 8
VMEM_LIMIT = 56 * 1024 * 1024

MASKED = -1.0e30
M_FLOOR = -1.0e20

ROW_TILE = 512
NSA_TILE = 256
MLA_TILE = 512
FF_CHUNK = 1024


def _dot(a, b):
    return jnp.dot(a, b, preferred_element_type=F32)


def _dot_nt(a, b):
    return lax.dot_general(a, b, (((1,), (1,)), ((), ())), preferred_element_type=F32)


def _rms(x, g):
    return x * lax.rsqrt(jnp.mean(x * x, axis=-1, keepdims=True) + EPS) * g


def _layout(spec):
    idx, scl = [], []
    for start, n, s in spec:
        if start is None:
            idx.extend([0] * n)
            scl.extend([0.0] * n)
        else:
            idx.extend(range(start, start + n))
            scl.extend([s] * n)
    return np.asarray(idx, np.int32), np.asarray(scl, np.float32)


def _nkv_col(br, kvi, g):
    return 384 + br * 256 + kvi * 128 + g * 64


def _in_layout():
    spec = []
    for br in range(3):
        for g in range(2):
            for kvi in range(2):
                spec += [(_nkv_col(br, kvi, g), 64, 1.0)]
    spec += [(1170, 256, 1.0), (1426, 128, 1.0)]
    spec += [(None, 64, 0.0), (1554, 32, 1.0), (None, 32, 0.0)]
    spec += [(None, 64, 0.0), (1570, 16, -1.0), (1554, 16, 1.0), (None, 32, 0.0)]
    spec += [(1586, 256, 1.0)]
    return _layout(spec)


def _in_layout_t():
    spec = []
    for h in range(NSA_HEADS):
        spec += [(h * 64, 64, 0.125), (None, 64, 0.0)]
    for br in (1, 2):
        for g in range(2):
            spec += [(_nkv_col(br, 1, g), 64, 1.0)]
    for g in range(2):
        spec += [(1152 + g * 9, 9, 1.0), (None, LANES - 9, 0.0)]
    return _layout(spec)


def _mla_layouts():
    q, qr, k, v = [], [], [], []
    for h in range(MLA_HEADS):
        b = h * 96
        q += [(b, 96, 1.0), (None, 32, 0.0)]
        qr += [(None, 64, 0.0), (b + 80, 16, -1.0), (b + 64, 16, 1.0), (None, 32, 0.0)]
        k += [(h * 128, 64, 1.0), (None, 64, 0.0)]
        v += [(h * 128 + 64, 64, 1.0)]
    return _layout(q), _layout(qr), _layout(k), _layout(v)


def _take_cols(w, layout):
    idx, scl = layout
    return (jnp.take(w, idx, axis=1) * scl[None, :]).astype(BF16)


def _take_rows(w, layout):
    idx, scl = layout
    return (jnp.take(w, idx, axis=0) * scl[:, None]).astype(BF16)


def _blockdiag2(a, b):
    z = jnp.zeros_like(a)
    top = jnp.concatenate([a, z], axis=-1)
    bot = jnp.concatenate([z, b], axis=-1)
    return jnp.concatenate([top, bot], axis=-2)


def _const_spec(shape):
    nd = len(shape)
    return pl.BlockSpec(shape, lambda *_: (0,) * nd, pipeline_mode=pl.Buffered(1))


def _params(sem):
    return pltpu.CompilerParams(dimension_semantics=sem, vmem_limit_bytes=VMEM_LIMIT)


def _rope_kernel(posc_ref, posr_ref, invr_ref, invc_ref, cos_ref, sin_ref, cost_ref, sint_ref):
    ang = posc_ref[...].astype(F32) * invr_ref[...]
    cos_ref[...] = jnp.cos(ang)
    sin_ref[...] = jnp.sin(ang)
    ang_t = invc_ref[...] * posr_ref[...].astype(F32)
    cost_ref[...] = jnp.cos(ang_t)
    sint_ref[...] = jnp.sin(ang_t)


def _rope_tables(positions):
    n = positions.size
    half = MLA_ROPE // 2
    inv = jnp.power(jnp.float32(ROPE_THETA), -(jnp.arange(half, dtype=F32) / half))
    inv128 = jnp.zeros((LANES,), F32).at[64:64 + half].set(inv).at[64 + half:64 + 2 * half].set(inv)
    tm = ROW_TILE
    out = jax.ShapeDtypeStruct((n, LANES), F32)
    out_t = jax.ShapeDtypeStruct((LANES, n), F32)
    return pl.pallas_call(
        _rope_kernel,
        grid=(n // tm,),
        in_specs=[pl.BlockSpec((tm, 1), lambda i: (i, 0)), pl.BlockSpec((1, tm), lambda i: (0, i)),
                  _const_spec((1, LANES)), _const_spec((LANES, 1))],
        out_specs=[pl.BlockSpec((tm, LANES), lambda i: (i, 0))] * 2 + [pl.BlockSpec((LANES, tm), lambda i: (0, i))] * 2,
        out_shape=[out, out, out_t, out_t],
        compiler_params=_params(("parallel",)),
        name="rope_tables",
    )(positions.reshape(n, 1), positions.reshape(1, n), inv128.reshape(1, LANES), inv128.reshape(LANES, 1))


def _proj_in_kernel(x_ref, g_ref, w_ref, wt_ref, cos_ref, sin_ref, cost_ref, sint_ref, qg_ref, kvg_ref,
                    wqt_ref, wqrt_ref, wk_ref, wvt_ref,
                    q6t_ref, vst_ref, vwt_ref, gatet_ref, kvc_ref, kvs_ref, kvw_ref, mqt_ref, mk_ref, mvt_ref, u_ref):
    h = _rms(x_ref[...], g_ref[...]).astype(BF16)
    zt = _dot_nt(wt_ref[...], h)
    q6t_ref[...] = zt[0:768].astype(BF16)
    vst_ref[...] = zt[768:896].astype(BF16)
    vwt_ref[...] = zt[896:1024].astype(BF16)
    gatet_ref[...] = jax.nn.sigmoid(zt[1024:1280])
    z = _dot(h, w_ref[:, 0:768])
    kvc_ref[...] = z[:, 0:256]
    kvs_ref[...] = z[:, 256:512].astype(BF16)
    kvw_ref[...] = z[:, 512:768].astype(BF16)
    z = _dot(h, w_ref[:, 768:1664])
    u_ref[...] = z[:, 640:896]
    cqn = _rms(z[:, 0:256], qg_ref[...]).astype(BF16)
    qa = _dot_nt(wqt_ref[...], cqn)
    qb = _dot_nt(wqrt_ref[...], cqn)
    cos_t = cost_ref[...]
    sin_t = sint_ref[...]
    scale = (MLA_NOPE + MLA_ROPE) ** -0.5
    for hh in range(MLA_HEADS):
        sl = slice(hh * LANES, (hh + 1) * LANES)
        mqt_ref[sl, :] = ((qa[sl] * cos_t + qb[sl] * sin_t) * scale).astype(BF16)
    ckvn = _rms(z[:, 256:384], kvg_ref[...]).astype(BF16)
    k_rope = z[:, 384:512] * cos_ref[...] + z[:, 512:640] * sin_ref[...]
    ka = _dot(ckvn, wk_ref[...])
    for hh in range(MLA_HEADS):
        sl = slice(hh * LANES, (hh + 1) * LANES)
        mk_ref[:, sl] = (ka[:, sl] + k_rope).astype(BF16)
    mvt_ref[...] = _dot_nt(wvt_ref[...], ckvn).astype(BF16)


def _proj_in(x, g, w_all, wt_all, cos, sin, cos_t, sin_t, qg, kvg, wqt, wqrt, wk, wvt):
    n = x.shape[0]
    tm = ROW_TILE
    row = lambda width: pl.BlockSpec((tm, width), lambda i: (i, 0))
    col = lambda height: pl.BlockSpec((height, tm), lambda i: (0, i))
    outs = [("t", 768, BF16), ("t", 128, BF16), ("t", 128, BF16), ("t", 256, F32), ("r", 256, F32), ("r", 256, BF16),
            ("r", 256, BF16), ("t", 768, BF16), ("r", 768, BF16), ("t", 384, BF16), ("r", 256, F32)]
    return pl.pallas_call(
        _proj_in_kernel,
        grid=(n // tm,),
        in_specs=[row(D_MODEL), _const_spec(g.shape), _const_spec(w_all.shape), _const_spec(wt_all.shape),
                  row(LANES), row(LANES), col(LANES), col(LANES), _const_spec(qg.shape), _const_spec(kvg.shape),
                  _const_spec(wqt.shape), _const_spec(wqrt.shape), _const_spec(wk.shape), _const_spec(wvt.shape)],
        out_specs=[col(w) if kind == "t" else row(w) for kind, w, _ in outs],
        out_shape=[jax.ShapeDtypeStruct((w, n) if kind == "t" else (n, w), dt) for kind, w, dt in outs],
        compiler_params=_params(("parallel",)),
        name="proj_in",
    )(x, g, w_all, wt_all, cos, sin, cos_t, sin_t, qg, kvg, wqt, wqrt, wk, wvt)


def _compress_kernel(kvc_ref, posa_ref, posb_ref, wa_ref, wb_ref, w2_ref, o_ref, vt_ref, *, ncp):
    a = jnp.zeros((ncp, LANES), F32)
    b = jnp.zeros((ncp, LANES), F32)
    for l in range(NSA_CMP_STRIDE):
        xl = kvc_ref[pl.ds(l, ncp, stride=NSA_CMP_STRIDE), :]
        a = a + _dot((xl + posa_ref[l]).astype(BF16), wa_ref[l])
        b = b + _dot((xl + posb_ref[l]).astype(BF16), wb_ref[l])
    pre = a + pltpu.roll(b, ncp - 1, axis=0)
    hid = jax.nn.gelu(pre).astype(BF16)
    out = _dot(hid, w2_ref[...])
    o_ref[...] = out.astype(BF16)
    vt_ref[...] = out.T[NSA_DH:2 * NSA_DH].astype(BF16)


def _compress(kvc, posa, posb, wa, wb, w2, batch, seq):
    ncp = seq // NSA_CMP_STRIDE
    return pl.pallas_call(
        functools.partial(_compress_kernel, ncp=ncp),
        grid=(batch, NSA_KV_HEADS),
        in_specs=[pl.BlockSpec((seq, LANES), lambda b, g: (b, g)), _const_spec(posa.shape), _const_spec(posb.shape),
                  _const_spec(wa.shape), _const_spec(wb.shape), _const_spec(w2.shape)],
        out_specs=[pl.BlockSpec((ncp, LANES), lambda b, g: (b, g)),
                   pl.BlockSpec((NSA_DH, ncp), lambda b, g: (b * NSA_KV_HEADS + g, 0))],
        out_shape=[jax.ShapeDtypeStruct((batch * ncp, 2 * LANES), BF16),
                   jax.ShapeDtypeStruct((batch * NSA_KV_HEADS * NSA_DH, ncp), BF16)],
        compiler_params=_params(("parallel", "parallel")),
        name="nsa_compress",
    )(kvc, posa, posb, wa, wb, w2)


def _flash_update(s, v_t, m, l, acc_ref):
    m_new = jnp.maximum(m, jnp.max(s, axis=0, keepdims=True))
    alpha = jnp.exp(m - m_new)
    p = jnp.exp(s - m_new)
    l_new = alpha * l + jnp.sum(p, axis=0, keepdims=True)
    acc_ref[...] = alpha * acc_ref[...] + _dot(v_t, p.astype(BF16))
    return m_new, l_new


def _flash_start(acc_ref, width):
    acc_ref[...] = jnp.zeros(acc_ref.shape, F32)
    return jnp.full((1, width), M_FLOOR, F32), jnp.zeros((1, width), F32)


def _nsa_kernel(qt_ref, kvc_ref, vct_ref, kvs_ref, vst_ref, kvw_ref, vwt_ref, gt_ref, ovt_ref, o_ref,
                bias_sc, acc_sc, *, tq, n_slc, n_sel):
    i = pl.program_id(2)
    q0 = pl.multiple_of(i * tq, tq)
    width = NSA_HPG * tq
    q3 = jnp.concatenate([qt_ref[h * LANES:(h + 1) * LANES, :] for h in range(NSA_HPG)], axis=1)

    kvc = kvc_ref[...]
    ncp = kvc.shape[0]
    s = _dot(kvc, q3)
    nblk = lax.broadcasted_iota(jnp.int32, (ncp, width), 0)
    qpos = q0 + (lax.broadcasted_iota(jnp.int32, (ncp, width), 1) & (tq - 1))
    s = jnp.where(nblk * NSA_CMP_STRIDE + (NSA_CMP_LEN - 1) <= qpos, s, -jnp.inf)
    m = jnp.max(s, axis=0, keepdims=True)
    m = jnp.where(m == -jnp.inf, 0.0, m)
    e = jnp.exp(s - m)
    d = jnp.sum(e, axis=0, keepdims=True)
    p = e * (1.0 / jnp.where(d > 0.0, d, 1.0))
    o_cmp = _dot(vct_ref[...], p.astype(BF16))

    psum = p[:, 0:tq] + p[:, tq:2 * tq] + p[:, 2 * tq:3 * tq]
    hi = psum.astype(BF16)
    lo = (psum - hi.astype(F32)).astype(BF16)
    ovt = ovt_ref[...]
    imp = _dot(ovt, hi) + _dot(ovt, lo)
    blk = lax.broadcasted_iota(jnp.int32, (LANES, tq), 0)
    qp = q0 + lax.broadcasted_iota(jnp.int32, (LANES, tq), 1)
    cur = qp >> 6
    forced = (blk == 0) | (blk == cur) | (blk == cur - 1)
    valid = (blk << 6) <= qp
    score = jnp.where(forced, NSA_FORCE, jnp.where(valid, imp, -1.0))[0:n_slc]
    sub = SUBLANES
    sidx = lax.broadcasted_iota(jnp.int32, (sub, tq), 0)
    groups = [score[k * sub:(k + 1) * sub] for k in range(n_slc // sub)]
    cnts = [jnp.zeros((sub, tq), F32) for _ in groups]
    for sp in range(n_slc):
        col = score[sp:sp + 1, :]
        for k, grp in enumerate(groups):
            if k * sub > sp:
                cnts[k] = cnts[k] + jnp.where(col >= grp, 1.0, 0.0)
            elif (k + 1) * sub - 1 <= sp:
                cnts[k] = cnts[k] + jnp.where(col > grp, 1.0, 0.0)
            else:
                after = jnp.where(col >= grp, 1.0, 0.0)
                cnts[k] = cnts[k] + jnp.where(sidx + k * sub > sp, after, jnp.where(col > grp, 1.0, 0.0))
    cnt = jnp.concatenate(cnts, axis=0)
    bias = jnp.where(cnt < n_sel, 0.0, MASKED)
    bias_sc[0:n_slc, :] = jnp.concatenate([bias] * NSA_HPG, axis=1)

    cc = lax.broadcasted_iota(jnp.int32, (tq, width), 0)
    rr = lax.broadcasted_iota(jnp.int32, (tq, width), 1) & (tq - 1)
    bpt = tq // NSA_SLC_LEN

    def slc_scores(j):
        off = pl.multiple_of(j * tq, tq)
        s = _dot(kvs_ref[pl.ds(off, tq), :], q3)
        parts = [s[b * NSA_SLC_LEN:(b + 1) * NSA_SLC_LEN] + bias_sc[pl.ds(j * bpt + b, 1), :] for b in range(bpt)]
        return jnp.concatenate(parts, axis=0), vst_ref[:, pl.ds(off, tq)]

    def slc_body(j, carry):
        s, v_t = slc_scores(j)
        return _flash_update(s, v_t, *carry, acc_sc)

    m, l = lax.fori_loop(0, i, slc_body, _flash_start(acc_sc, width))
    s, v_t = slc_scores(i)
    m, l = _flash_update(jnp.where(cc <= rr, s, MASKED), v_t, m, l, acc_sc)
    o_slc = acc_sc[...] * (1.0 / l)

    def win_body(j, carry):
        off = pl.multiple_of(j * tq, tq)
        s = _dot(kvw_ref[pl.ds(off, tq), :], q3)
        s = jnp.where((q0 + rr) - (off + cc) < NSA_WINDOW, s, MASKED)
        return _flash_update(s, vwt_ref[:, pl.ds(off, tq)], *carry, acc_sc)

    m, l = lax.fori_loop(jnp.maximum(i - NSA_WINDOW // tq, 0), i, win_body, _flash_start(acc_sc, width))
    s = _dot(kvw_ref[pl.ds(q0, tq), :], q3)
    m, l = _flash_update(jnp.where(cc <= rr, s, MASKED), vwt_ref[:, pl.ds(q0, tq)], m, l, acc_sc)
    o_win = acc_sc[...] * (1.0 / l)

    gt = gt_ref[...]
    outs = []
    for h in range(NSA_HPG):
        sl = slice(h * tq, (h + 1) * tq)
        outs.append(gt[3 * h:3 * h + 1] * o_cmp[:, sl] + gt[3 * h + 1:3 * h + 2] * o_slc[:, sl]
                    + gt[3 * h + 2:3 * h + 3] * o_win[:, sl])
    outs.append(jnp.zeros((NSA_DH, tq), F32))
    o_ref[...] = jnp.concatenate(outs, axis=0).T.astype(BF16)


def _nsa_attention(q6t, kvcmp, vct, kvs, vst, kvw, vwt, gate_t, ovt, batch, seq):
    tq = NSA_TILE
    nq = seq // tq
    ncp = seq // NSA_CMP_STRIDE
    n_slc = seq // NSA_SLC_LEN
    width = NSA_HPG * tq
    kern = functools.partial(_nsa_kernel, tq=tq, n_slc=n_slc, n_sel=min(NSA_TOPN, n_slc))
    kv_spec = pl.BlockSpec((seq, LANES), lambda b, g, i: (b, g))
    vt_spec = pl.BlockSpec((NSA_DH, seq), lambda b, g, i: (g, b))
    return pl.pallas_call(
        kern,
        grid=(batch, NSA_KV_HEADS, nq),
        in_specs=[pl.BlockSpec((NSA_HPG * LANES, tq), lambda b, g, i: (g, b * nq + i)),
                  pl.BlockSpec((ncp, LANES), lambda b, g, i: (b, g)),
                  pl.BlockSpec((NSA_DH, ncp), lambda b, g, i: (b * NSA_KV_HEADS + g, 0)),
                  kv_spec, vt_spec, kv_spec, vt_spec,
                  pl.BlockSpec((LANES, tq), lambda b, g, i: (g, b * nq + i)),
                  _const_spec(ovt.shape)],
        out_specs=pl.BlockSpec((tq, 2 * LANES), lambda b, g, i: (b * nq + i, g)),
        out_shape=jax.ShapeDtypeStruct((batch * seq, NSA_KV_HEADS * 2 * LANES), BF16),
        scratch_shapes=[pltpu.VMEM((LANES, width), F32), pltpu.VMEM((NSA_DH, width), F32)],
        compiler_params=_params(("parallel", "parallel", "arbitrary")),
        name="nsa_attention",
    )(q6t, kvcmp, vct, kvs, vst, kvw, vwt, gate_t, ovt)


def _mla_kernel(qt_ref, k_ref, vt_ref, o_ref, acc0_sc, acc1_sc, *, tq):
    i = pl.program_id(2)
    accs = (acc0_sc, acc1_sc)
    cc = lax.broadcasted_iota(jnp.int32, (tq, tq), 0)
    rr = lax.broadcasted_iota(jnp.int32, (tq, tq), 1)

    def scores(j, hh):
        off = pl.multiple_of(j * tq, tq)
        s = _dot(k_ref[pl.ds(off, tq), hh * LANES:(hh + 1) * LANES], qt_ref[hh * LANES:(hh + 1) * LANES, :])
        return s, vt_ref[hh * MLA_VDIM:(hh + 1) * MLA_VDIM, pl.ds(off, tq)]

    def body(j, carry):
        new = []
        for hh in range(2):
            s, v_t = scores(j, hh)
            new.append(_flash_update(s, v_t, *carry[hh], accs[hh]))
        return tuple(new)

    carry = lax.fori_loop(0, i, body, tuple(_flash_start(accs[hh], tq) for hh in range(2)))
    outs = []
    for hh in range(2):
        s, v_t = scores(i, hh)
        _, l = _flash_update(jnp.where(cc <= rr, s, MASKED), v_t, *carry[hh], accs[hh])
        outs.append(accs[hh][...] * (1.0 / l))
    o_ref[...] = jnp.concatenate(outs, axis=0).T.astype(BF16)


def _mla_attention(mqt, mk, mvt, batch, seq):
    tq = MLA_TILE
    nq = seq // tq
    pairs = MLA_HEADS // 2
    return pl.pallas_call(
        functools.partial(_mla_kernel, tq=tq),
        grid=(batch, pairs, nq),
        in_specs=[pl.BlockSpec((2 * LANES, tq), lambda b, p, i: (p, b * nq + i)),
                  pl.BlockSpec((seq, 2 * LANES), lambda b, p, i: (b, p)),
                  pl.BlockSpec((2 * MLA_VDIM, seq), lambda b, p, i: (p, b))],
        out_specs=pl.BlockSpec((tq, LANES), lambda b, p, i: (b * nq + i, p)),
        out_shape=jax.ShapeDtypeStruct((batch * seq, pairs * LANES), BF16),
        scratch_shapes=[pltpu.VMEM((MLA_VDIM, tq), F32), pltpu.VMEM((MLA_VDIM, tq), F32)],
        compiler_params=_params(("parallel", "parallel", "arbitrary")),
        name="mla_attention",
    )(mqt, mk, mvt)


def _out_ffn_kernel(x_ref, on_ref, om_ref, u_ref, uh_ref, pw_ref, ps_ref, won_ref, wom_ref, wop_ref, g2_ref,
                    w1_ref, w2_ref, gf_ref, y_ref, *, tm, seq, final):
    i = pl.program_id(0)
    t0 = (i * tm) % seq
    u = u_ref[...]
    halo = jnp.where(t0 == 0, 0.0, uh_ref[...])
    ext = jnp.concatenate([halo, u], axis=0)
    s2 = ext + pltpu.roll(ext, 1, axis=0)
    s4 = s2 + pltpu.roll(s2, 2, axis=0)
    s8 = s4 + pltpu.roll(s4, 4, axis=0)
    s16 = s8 + pltpu.roll(s8, 8, axis=0)
    sl = slice(POOL_HALO, POOL_HALO + tm)
    lane = lax.broadcasted_iota(jnp.int32, (tm, POOL_WIDTH), 1)
    t = t0 + lax.broadcasted_iota(jnp.int32, (tm, POOL_WIDTH), 0)
    sums = jnp.where(lane < 64, s2[sl], jnp.where(lane < 128, s4[sl], jnp.where(lane < 192, s8[sl], s16[sl])))
    win = jnp.where(lane < 64, 2, jnp.where(lane < 128, 4, jnp.where(lane < 192, 8, 16)))
    cnt = jnp.minimum(t + 1, win).astype(F32)
    pooled = (sums / cnt - u).astype(BF16)
    y_pool = (_dot(pooled, pw_ref[...]) * ps_ref[...]).astype(BF16)
    mix = _dot(on_ref[...], won_ref[...]) + _dot(om_ref[...], wom_ref[...]) + _dot(y_pool, wop_ref[...])
    x = x_ref[...] + mix
    h = _rms(x, g2_ref[...]).astype(BF16)
    ff = None
    for c in range(D_FF // FF_CHUNK):
        cs = slice(c * FF_CHUNK, (c + 1) * FF_CHUNK)
        a = jnp.maximum(_dot(h, w1_ref[:, cs]), 0.0)
        part = _dot((a * a).astype(BF16), w2_ref[cs, :])
        ff = part if ff is None else ff + part
    acc = x + ff
    if final:
        acc = _rms(acc, gf_ref[...])
    y_ref[...] = acc


def _out_ffn(x, o_nsa, o_mla, u, pw, ps, won, wom, wop, g2, w1, w2, gf, seq, final):
    n = x.shape[0]
    tm = ROW_TILE
    hb = tm // POOL_HALO
    row = lambda width: pl.BlockSpec((tm, width), lambda i: (i, 0))
    return pl.pallas_call(
        functools.partial(_out_ffn_kernel, tm=tm, seq=seq, final=final),
        grid=(n // tm,),
        in_specs=[row(D_MODEL), row(o_nsa.shape[1]), row(o_mla.shape[1]), row(POOL_WIDTH),
                  pl.BlockSpec((POOL_HALO, POOL_WIDTH), lambda i: (jnp.maximum(i * hb - 1, 0), 0)),
                  _const_spec(pw.shape), _const_spec(ps.shape), _const_spec(won.shape), _const_spec(wom.shape),
                  _const_spec(wop.shape), _const_spec(g2.shape), _const_spec(w1.shape), _const_spec(w2.shape),
                  _const_spec(gf.shape)],
        out_specs=row(D_MODEL),
        out_shape=jax.ShapeDtypeStruct((n, D_MODEL), F32),
        compiler_params=_params(("parallel",)),
        name="out_ffn",
    )(x, o_nsa, o_mla, u, u, pw, ps, won, wom, wop, g2, w1, w2, gf)


def _overlap_t(seq):
    ncp = seq // NSA_CMP_STRIDE
    n_slc = seq // NSA_SLC_LEN
    start = np.arange(ncp - 1)[None, :] * NSA_CMP_STRIDE
    end = start + NSA_CMP_LEN
    s0 = np.arange(n_slc)[:, None] * NSA_SLC_LEN
    s1 = s0 + NSA_SLC_LEN
    ov = np.clip(np.minimum(end, s1) - np.maximum(start, s0), 0, None) / NSA_CMP_LEN
    out = np.zeros((LANES, ncp), np.float32)
    out[:n_slc, :ncp - 1] = ov
    return jnp.asarray(out, BF16)


def kernel(x, positions, ln1_g, w_in, nsa_cmp_pos, nsa_cmp_w1, nsa_cmp_w2, mla_q_norm, mla_w_qup, mla_kv_norm,
           mla_w_kvup, pool_w, pool_scale, w_out, ln2_g, w_ff1, w_ff2, final_g):
    batch, seq, _ = x.shape
    depth = w_in.shape[0]
    n = batch * seq
    assert n % ROW_TILE == 0 and seq % ROW_TILE == 0 and seq % NSA_TILE == 0 and seq % MLA_TILE == 0
    assert seq // NSA_SLC_LEN <= LANES and NSA_WINDOW % NSA_TILE == 0 and NSA_TILE % NSA_SLC_LEN == 0

    cos, sin, cos_t, sin_t = _rope_tables(positions)
    ovt = _overlap_t(seq)
    in_layout, in_layout_t = _in_layout(), _in_layout_t()
    lq, lqr, lk, lv = _mla_layouts()
    won_layout = _layout([r for g in range(NSA_KV_HEADS)
                          for r in ((g * NSA_HPG * NSA_DH, NSA_HPG * NSA_DH, 1.0), (None, NSA_DH, 0.0))])
    row2 = lambda v: v.reshape(1, -1)

    xf = x.reshape(n, D_MODEL)
    for l in range(depth):
        w_all = _take_cols(w_in[l], in_layout)
        wt_all = _take_cols(w_in[l], in_layout_t).T
        wqt, wqrt = _take_cols(mla_w_qup[l], lq).T, _take_cols(mla_w_qup[l], lqr).T
        wk, wvt = _take_cols(mla_w_kvup[l], lk), _take_cols(mla_w_kvup[l], lv).T
        q6t, vst, vwt, gate_t, kvc, kvs, kvw, mqt, mk, mvt, u = _proj_in(
            xf, row2(ln1_g[l]), w_all, wt_all, cos, sin, cos_t, sin_t, row2(mla_q_norm[l]), row2(mla_kv_norm[l]),
            wqt, wqrt, wk, wvt)

        w1c = _blockdiag2(nsa_cmp_w1[l, 0], nsa_cmp_w1[l, 1]).astype(BF16)
        w2c = _blockdiag2(nsa_cmp_w2[l, 0], nsa_cmp_w2[l, 1]).astype(BF16)
        pos = jnp.concatenate([nsa_cmp_pos[l, 0], nsa_cmp_pos[l, 1]], axis=-1)[:, None, :]
        half = NSA_CMP_STRIDE
        kvcmp, vct = _compress(kvc, pos[:half], pos[half:], w1c[:half], w1c[half:], w2c, batch, seq)

        o_nsa = _nsa_attention(q6t, kvcmp, vct, kvs, vst, kvw, vwt, gate_t, ovt, batch, seq)
        o_mla = _mla_attention(mqt, mk, mvt, batch, seq)

        pw = _blockdiag2(_blockdiag2(pool_w[l, 0], pool_w[l, 1]), _blockdiag2(pool_w[l, 2], pool_w[l, 3])).astype(BF16)
        won = _take_rows(w_out[l], won_layout)
        wom = w_out[l, 384:768].astype(BF16)
        wop = w_out[l, 768:1024].astype(BF16)
        xf = _out_ffn(xf, o_nsa, o_mla, u, pw, row2(pool_scale[l]), won, wom, wop, row2(ln2_g[l]),
                      w_ff1[l].astype(BF16), w_ff2[l].astype(BF16), row2(final_g), seq, final=(l == depth - 1))
    return xf.reshape(batch, seq, D_MODEL)
```

```python
import functools

import numpy as np
import jax
import jax.numpy as jnp
from jax import lax
from jax.experimental import pallas as pl
from jax.experimental.pallas import tpu as pltpu

F32 = jnp.float32
BF16 = jnp.bfloat16

D_MODEL = 1024
NSA_HEADS = 6
NSA_KV_HEADS = 2
NSA_HPG = 3
NSA_DH = 64
NSA_CMP_LEN = 32
NSA_CMP_STRIDE = 16
NSA_SLC_LEN = 64
NSA_TOPN = 16
NSA_WINDOW = 512
NSA_FORCE = 1.0e4
MLA_HEADS = 6
MLA_NOPE = 64
MLA_ROPE = 32
MLA_VDIM = 64
ROPE_THETA = 10000.0
POOL_WIDTH = 256
POOL_HALO = 16
D_FF = 4096
EPS = 1e-6

LANES = 128
SUBLANES = 8
VMEM_LIMIT = 56 * 1024 * 1024

MASKED = -1.0e30
M_FLOOR = -1.0e20
LOG2E = 1.4426950408889634

ROW_TILE = 512
NSA_TILE = 256
MLA_TILE = 512
FF_CHUNK = 1024


def _dot(a, b):
    return jnp.dot(a, b, preferred_element_type=F32)


def _dot_nt(a, b):
    return lax.dot_general(a, b, (((1,), (1,)), ((), ())), preferred_element_type=F32)


def _rms(x, g):
    return x * lax.rsqrt(jnp.mean(x * x, axis=-1, keepdims=True) + EPS) * g


def _layout(spec):
    idx, scl = [], []
    for start, n, s in spec:
        if start is None:
            idx.extend([0] * n)
            scl.extend([0.0] * n)
        else:
            idx.extend(range(start, start + n))
            scl.extend([s] * n)
    return np.asarray(idx, np.int32), np.asarray(scl, np.float32)


def _nkv_col(br, kvi, g):
    return 384 + br * 256 + kvi * 128 + g * 64


def _in_layout():
    spec = []
    for br in range(3):
        for g in range(2):
            for kvi in range(2):
                spec += [(_nkv_col(br, kvi, g), 64, 1.0)]
    spec += [(1170, 256, 1.0), (1426, 128, 1.0)]
    spec += [(None, 64, 0.0), (1554, 32, 1.0), (None, 32, 0.0)]
    spec += [(None, 64, 0.0), (1570, 16, -1.0), (1554, 16, 1.0), (None, 32, 0.0)]
    spec += [(1586, 256, 1.0)]
    return _layout(spec)


def _in_layout_t():
    spec = []
    for h in range(NSA_HEADS):
        spec += [(h * 64, 64, 0.125), (None, 64, 0.0)]
    for br in (1, 2):
        for g in range(2):
            spec += [(_nkv_col(br, 1, g), 64, 1.0)]
    for g in range(2):
        spec += [(1152 + g * 9, 9, 1.0), (None, LANES - 9, 0.0)]
    return _layout(spec)


def _mla_layouts():
    q, qr, k, v = [], [], [], []
    for h in range(MLA_HEADS):
        b = h * 96
        q += [(b, 96, 1.0), (None, 32, 0.0)]
        qr += [(None, 64, 0.0), (b + 80, 16, -1.0), (b + 64, 16, 1.0), (None, 32, 0.0)]
        k += [(h * 128, 64, 1.0), (None, 64, 0.0)]
        v += [(h * 128 + 64, 64, 1.0)]
    return _layout(q), _layout(qr), _layout(k), _layout(v)


def _take_cols(w, layout):
    idx, scl = layout
    return (jnp.take(w, idx, axis=1) * scl[None, :]).astype(BF16)


def _take_rows(w, layout):
    idx, scl = layout
    return (jnp.take(w, idx, axis=0) * scl[:, None]).astype(BF16)


def _blockdiag2(a, b):
    z = jnp.zeros_like(a)
    top = jnp.concatenate([a, z], axis=-1)
    bot = jnp.concatenate([z, b], axis=-1)
    return jnp.concatenate([top, bot], axis=-2)


def _const_spec(shape):
    nd = len(shape)
    return pl.BlockSpec(shape, lambda *_: (0,) * nd, pipeline_mode=pl.Buffered(1))


def _params(sem):
    return pltpu.CompilerParams(dimension_semantics=sem, vmem_limit_bytes=VMEM_LIMIT)


def _rope_kernel(posc_ref, posr_ref, invr_ref, invc_ref, cos_ref, sin_ref, cost_ref, sint_ref):
    ang = posc_ref[...].astype(F32) * invr_ref[...]
    cos_ref[...] = jnp.cos(ang)
    sin_ref[...] = jnp.sin(ang)
    ang_t = invc_ref[...] * posr_ref[...].astype(F32)
    cost_ref[...] = jnp.cos(ang_t)
    sint_ref[...] = jnp.sin(ang_t)


def _rope_tables(positions):
    n = positions.size
    half = MLA_ROPE // 2
    inv = jnp.power(jnp.float32(ROPE_THETA), -(jnp.arange(half, dtype=F32) / half))
    inv128 = jnp.zeros((LANES,), F32).at[64:64 + half].set(inv).at[64 + half:64 + 2 * half].set(inv)
    tm = ROW_TILE
    out = jax.ShapeDtypeStruct((n, LANES), F32)
    out_t = jax.ShapeDtypeStruct((LANES, n), F32)
    return pl.pallas_call(
        _rope_kernel,
        grid=(n // tm,),
        in_specs=[pl.BlockSpec((tm, 1), lambda i: (i, 0)), pl.BlockSpec((1, tm), lambda i: (0, i)),
                  _const_spec((1, LANES)), _const_spec((LANES, 1))],
        out_specs=[pl.BlockSpec((tm, LANES), lambda i: (i, 0))] * 2 + [pl.BlockSpec((LANES, tm), lambda i: (0, i))] * 2,
        out_shape=[out, out, out_t, out_t],
        compiler_params=_params(("parallel",)),
        name="rope_tables",
    )(positions.reshape(n, 1), positions.reshape(1, n), inv128.reshape(1, LANES), inv128.reshape(LANES, 1))


def _proj_in_kernel(x_ref, g_ref, w_ref, wt_ref, cos_ref, sin_ref, cost_ref, sint_ref, qg_ref, kvg_ref,
                    wqt_ref, wqrt_ref, wk_ref, wvt_ref,
                    q6t_ref, vst_ref, vwt_ref, gatet_ref, kvc_ref, kvs_ref, kvw_ref, mqt_ref, mk_ref, mvt_ref, u_ref):
    h = _rms(x_ref[...], g_ref[...]).astype(BF16)
    zt = _dot_nt(wt_ref[...], h)
    q6t_ref[...] = (zt[0:768] * LOG2E).astype(BF16)
    vst_ref[...] = zt[768:896].astype(BF16)
    vwt_ref[...] = zt[896:1024].astype(BF16)
    gatet_ref[...] = jax.nn.sigmoid(zt[1024:1280])
    z = _dot(h, w_ref[:, 0:768])
    kvc_ref[...] = z[:, 0:256]
    kvs_ref[...] = z[:, 256:512].astype(BF16)
    kvw_ref[...] = z[:, 512:768].astype(BF16)
    z = _dot(h, w_ref[:, 768:1664])
    u_ref[...] = z[:, 640:896]
    cqn = _rms(z[:, 0:256], qg_ref[...]).astype(BF16)
    qa = _dot_nt(wqt_ref[...], cqn)
    qb = _dot_nt(wqrt_ref[...], cqn)
    cos_t = cost_ref[...]
    sin_t = sint_ref[...]
    scale = (MLA_NOPE + MLA_ROPE) ** -0.5 * LOG2E
    for hh in range(MLA_HEADS):
        sl = slice(hh * LANES, (hh + 1) * LANES)
        mqt_ref[sl, :] = ((qa[sl] * cos_t + qb[sl] * sin_t) * scale).astype(BF16)
    ckvn = _rms(z[:, 256:384], kvg_ref[...]).astype(BF16)
    k_rope = z[:, 384:512] * cos_ref[...] + z[:, 512:640] * sin_ref[...]
    ka = _dot(ckvn, wk_ref[...])
    for hh in range(MLA_HEADS):
        sl = slice(hh * LANES, (hh + 1) * LANES)
        mk_ref[:, sl] = (ka[:, sl] + k_rope).astype(BF16)
    mvt_ref[...] = _dot_nt(wvt_ref[...], ckvn).astype(BF16)


def _proj_in(x, g, w_all, wt_all, cos, sin, cos_t, sin_t, qg, kvg, wqt, wqrt, wk, wvt):
    n = x.shape[0]
    tm = ROW_TILE
    row = lambda width: pl.BlockSpec((tm, width), lambda i: (i, 0))
    col = lambda height: pl.BlockSpec((height, tm), lambda i: (0, i))
    outs = [("t", 768, BF16), ("t", 128, BF16), ("t", 128, BF16), ("t", 256, F32), ("r", 256, F32), ("r", 256, BF16),
            ("r", 256, BF16), ("t", 768, BF16), ("r", 768, BF16), ("t", 384, BF16), ("r", 256, F32)]
    return pl.pallas_call(
        _proj_in_kernel,
        grid=(n // tm,),
        in_specs=[row(D_MODEL), _const_spec(g.shape), _const_spec(w_all.shape), _const_spec(wt_all.shape),
                  row(LANES), row(LANES), col(LANES), col(LANES), _const_spec(qg.shape), _const_spec(kvg.shape),
                  _const_spec(wqt.shape), _const_spec(wqrt.shape), _const_spec(wk.shape), _const_spec(wvt.shape)],
        out_specs=[col(w) if kind == "t" else row(w) for kind, w, _ in outs],
        out_shape=[jax.ShapeDtypeStruct((w, n) if kind == "t" else (n, w), dt) for kind, w, dt in outs],
        compiler_params=_params(("parallel",)),
        name="proj_in",
    )(x, g, w_all, wt_all, cos, sin, cos_t, sin_t, qg, kvg, wqt, wqrt, wk, wvt)


def _compress_kernel(kvc_ref, posa_ref, posb_ref, wa_ref, wb_ref, w2_ref, o_ref, vt_ref, *, ncp):
    a = jnp.zeros((ncp, LANES), F32)
    b = jnp.zeros((ncp, LANES), F32)
    for l in range(NSA_CMP_STRIDE):
        xl = kvc_ref[pl.ds(l, ncp, stride=NSA_CMP_STRIDE), :]
        a = a + _dot((xl + posa_ref[l]).astype(BF16), wa_ref[l])
        b = b + _dot((xl + posb_ref[l]).astype(BF16), wb_ref[l])
    pre = a + pltpu.roll(b, ncp - 1, axis=0)
    hid = jax.nn.gelu(pre).astype(BF16)
    out = _dot(hid, w2_ref[...])
    o_ref[...] = out.astype(BF16)
    vt_ref[...] = out.T[NSA_DH:2 * NSA_DH].astype(BF16)


def _compress(kvc, posa, posb, wa, wb, w2, batch, seq):
    ncp = seq // NSA_CMP_STRIDE
    return pl.pallas_call(
        functools.partial(_compress_kernel, ncp=ncp),
        grid=(batch, NSA_KV_HEADS),
        in_specs=[pl.BlockSpec((seq, LANES), lambda b, g: (b, g)), _const_spec(posa.shape), _const_spec(posb.shape),
                  _const_spec(wa.shape), _const_spec(wb.shape), _const_spec(w2.shape)],
        out_specs=[pl.BlockSpec((ncp, LANES), lambda b, g: (b, g)),
                   pl.BlockSpec((NSA_DH, ncp), lambda b, g: (b * NSA_KV_HEADS + g, 0))],
        out_shape=[jax.ShapeDtypeStruct((batch * ncp, 2 * LANES), BF16),
                   jax.ShapeDtypeStruct((batch * NSA_KV_HEADS * NSA_DH, ncp), BF16)],
        compiler_params=_params(("parallel", "parallel")),
        name="nsa_compress",
    )(kvc, posa, posb, wa, wb, w2)


def _flash_update(buf, v_t, m, l, acc_ref, mask=None):
    load = (lambda: buf[...]) if mask is None else (lambda: jnp.where(mask, buf[...], MASKED))
    m_new = jnp.maximum(m, jnp.max(load(), axis=0, keepdims=True))
    alpha = jnp.exp2(m - m_new)
    p = jnp.exp2(load() - m_new)
    l_new = alpha * l + jnp.sum(p, axis=0, keepdims=True)
    acc_ref[...] = alpha * acc_ref[...] + _dot(v_t, p.astype(BF16))
    return m_new, l_new


def _flash_start(acc_ref, width):
    acc_ref[...] = jnp.zeros(acc_ref.shape, F32)
    return jnp.full((1, width), M_FLOOR, F32), jnp.zeros((1, width), F32)


def _flash_tiles(first, last, scores, values, bufs, acc_ref, width, last_mask):
    buf_a, buf_b = bufs
    n = last - first
    buf_a[...] = scores(first)

    def pair(jj, carry):
        t = first + 2 * jj
        buf_b[...] = scores(t + 1)
        carry = _flash_update(buf_a, values(t), *carry, acc_ref)
        buf_a[...] = scores(t + 2)
        return _flash_update(buf_b, values(t + 1), *carry, acc_ref)

    def single(_, carry):
        carry = _flash_update(buf_a, values(last - 1), *carry, acc_ref)
        buf_a[...] = scores(last)
        return carry

    carry = lax.fori_loop(0, n // 2, pair, _flash_start(acc_ref, width))
    carry = lax.fori_loop(0, n % 2, single, carry)
    return _flash_update(buf_a, values(last), *carry, acc_ref, mask=last_mask)


def _nsa_kernel(qt_ref, kvc_ref, vct_ref, kvs_ref, vst_ref, kvw_ref, vwt_ref, gt_ref, ovt_ref, rep_ref, o_ref,
                bias_sc, acc_sc, sa_sc, sb_sc, *, tq, n_slc, n_sel):
    i = pl.program_id(2)
    q0 = pl.multiple_of(i * tq, tq)
    width = NSA_HPG * tq
    q3 = jnp.concatenate([qt_ref[h * LANES:(h + 1) * LANES, :] for h in range(NSA_HPG)], axis=1)

    kvc = kvc_ref[...]
    ncp = kvc.shape[0]
    s = _dot(kvc, q3)
    nblk = lax.broadcasted_iota(jnp.int32, (ncp, width), 0)
    qpos = q0 + (lax.broadcasted_iota(jnp.int32, (ncp, width), 1) & (tq - 1))
    s = jnp.where(nblk * NSA_CMP_STRIDE + (NSA_CMP_LEN - 1) <= qpos, s, -jnp.inf)
    m = jnp.max(s, axis=0, keepdims=True)
    m = jnp.where(m == -jnp.inf, 0.0, m)
    e = jnp.exp2(s - m)
    d = jnp.sum(e, axis=0, keepdims=True)
    p = e * (1.0 / jnp.where(d > 0.0, d, 1.0))
    o_cmp = _dot(vct_ref[...], p.astype(BF16))

    every_valid_block_selected = q0 + tq <= n_sel * NSA_SLC_LEN

    @pl.when(every_valid_block_selected)
    def _():
        bias_sc[...] = jnp.zeros(bias_sc.shape, F32)

    @pl.when(jnp.logical_not(every_valid_block_selected))
    def _():
        psum = p[:, 0:tq] + p[:, tq:2 * tq] + p[:, 2 * tq:3 * tq]
        hi = psum.astype(BF16)
        lo = (psum - hi.astype(F32)).astype(BF16)
        ovt = ovt_ref[...]
        imp = _dot(ovt, hi) + _dot(ovt, lo)
        blk = lax.broadcasted_iota(jnp.int32, (LANES, tq), 0)
        qp = q0 + lax.broadcasted_iota(jnp.int32, (LANES, tq), 1)
        cur = qp >> 6
        forced = (blk == 0) | (blk == cur) | (blk == cur - 1)
        valid = (blk << 6) <= qp
        score = jnp.where(forced, NSA_FORCE, jnp.where(valid, imp, -1.0))[0:n_slc]
        sub = SUBLANES
        sidx = lax.broadcasted_iota(jnp.int32, (sub, tq), 0)
        groups = [score[k * sub:(k + 1) * sub] for k in range(n_slc // sub)]
        cnts = [jnp.zeros((sub, tq), F32) for _ in groups]
        for sp in range(n_slc):
            col = score[sp:sp + 1, :]
            for k, grp in enumerate(groups):
                if k * sub > sp:
                    cnts[k] = cnts[k] + jnp.where(col >= grp, 1.0, 0.0)
                elif (k + 1) * sub - 1 <= sp:
                    cnts[k] = cnts[k] + jnp.where(col > grp, 1.0, 0.0)
                else:
                    after = jnp.where(col >= grp, 1.0, 0.0)
                    cnts[k] = cnts[k] + jnp.where(sidx + k * sub > sp, after, jnp.where(col > grp, 1.0, 0.0))
        dropped = jnp.where(jnp.concatenate(cnts, axis=0) < n_sel, 0.0, 1.0).astype(BF16)
        bias_sc[...] = jnp.where(_dot(rep_ref[...], dropped) > 0.5, MASKED, 0.0)

    cc = lax.broadcasted_iota(jnp.int32, (tq, width), 0)
    rr = lax.broadcasted_iota(jnp.int32, (tq, width), 1) & (tq - 1)
    bpt = tq // NSA_SLC_LEN
    reps = NSA_SLC_LEN // SUBLANES

    def slc_scores(t):
        off = pl.multiple_of(t * tq, tq)
        s = _dot(kvs_ref[pl.ds(off, tq), :], q3)
        rows = []
        for b in range(bpt):
            row = bias_sc[pl.ds(pl.multiple_of((t * bpt + b) * SUBLANES, SUBLANES), SUBLANES), :]
            rows += [jnp.concatenate([row] * NSA_HPG, axis=1)] * reps
        return s + jnp.concatenate(rows, axis=0)

    def slc_values(t):
        return vst_ref[:, pl.ds(pl.multiple_of(t * tq, tq), tq)]

    _, l = _flash_tiles(0, i, slc_scores, slc_values, (sa_sc, sb_sc), acc_sc, width, cc <= rr)
    o_slc = acc_sc[...] * (1.0 / l)

    def win_scores(t):
        off = pl.multiple_of(t * tq, tq)
        s = _dot(kvw_ref[pl.ds(off, tq), :], q3)
        return jnp.where((q0 + rr) - (off + cc) < NSA_WINDOW, s, MASKED)

    def win_values(t):
        return vwt_ref[:, pl.ds(pl.multiple_of(t * tq, tq), tq)]

    first = jnp.maximum(i - NSA_WINDOW // tq, 0)
    _, l = _flash_tiles(first, i, win_scores, win_values, (sa_sc, sb_sc), acc_sc, width, cc <= rr)
    o_win = acc_sc[...] * (1.0 / l)

    gt = gt_ref[...]
    outs = []
    for h in range(NSA_HPG):
        sl = slice(h * tq, (h + 1) * tq)
        outs.append(gt[3 * h:3 * h + 1] * o_cmp[:, sl] + gt[3 * h + 1:3 * h + 2] * o_slc[:, sl]
                    + gt[3 * h + 2:3 * h + 3] * o_win[:, sl])
    outs.append(jnp.zeros((NSA_DH, tq), F32))
    o_ref[...] = jnp.concatenate(outs, axis=0).T.astype(BF16)


def _nsa_attention(q6t, kvcmp, vct, kvs, vst, kvw, vwt, gate_t, ovt, rep, batch, seq):
    tq = NSA_TILE
    nq = seq // tq
    ncp = seq // NSA_CMP_STRIDE
    n_slc = seq // NSA_SLC_LEN
    width = NSA_HPG * tq
    kern = functools.partial(_nsa_kernel, tq=tq, n_slc=n_slc, n_sel=min(NSA_TOPN, n_slc))
    kv_spec = pl.BlockSpec((seq, LANES), lambda b, g, i: (b, g))
    vt_spec = pl.BlockSpec((NSA_DH, seq), lambda b, g, i: (g, b))
    return pl.pallas_call(
        kern,
        grid=(batch, NSA_KV_HEADS, nq),
        in_specs=[pl.BlockSpec((NSA_HPG * LANES, tq), lambda b, g, i: (g, b * nq + i)),
                  pl.BlockSpec((ncp, LANES), lambda b, g, i: (b, g)),
                  pl.BlockSpec((NSA_DH, ncp), lambda b, g, i: (b * NSA_KV_HEADS + g, 0)),
                  kv_spec, vt_spec, kv_spec, vt_spec,
                  pl.BlockSpec((LANES, tq), lambda b, g, i: (g, b * nq + i)),
                  _const_spec(ovt.shape), _const_spec(rep.shape)],
        out_specs=pl.BlockSpec((tq, 2 * LANES), lambda b, g, i: (b * nq + i, g)),
        out_shape=jax.ShapeDtypeStruct((batch * seq, NSA_KV_HEADS * 2 * LANES), BF16),
        scratch_shapes=[pltpu.VMEM((n_slc * SUBLANES, tq), F32), pltpu.VMEM((NSA_DH, width), F32),
                        pltpu.VMEM((tq, width), F32), pltpu.VMEM((tq, width), F32)],
        compiler_params=_params(("parallel", "parallel", "arbitrary")),
        name="nsa_attention",
    )(q6t, kvcmp, vct, kvs, vst, kvw, vwt, gate_t, ovt, rep)


def _mla_kernel(qt_ref, k_ref, vt_ref, o_ref, acc_sc, sa_sc, sb_sc, *, tq):
    i = pl.program_id(2)
    causal = lax.broadcasted_iota(jnp.int32, (tq, tq), 0) <= lax.broadcasted_iota(jnp.int32, (tq, tq), 1)
    outs = []
    for hh in range(2):
        def scores(t, hh=hh):
            off = pl.multiple_of(t * tq, tq)
            return _dot(k_ref[pl.ds(off, tq), hh * LANES:(hh + 1) * LANES], qt_ref[hh * LANES:(hh + 1) * LANES, :])

        def values(t, hh=hh):
            return vt_ref[hh * MLA_VDIM:(hh + 1) * MLA_VDIM, pl.ds(pl.multiple_of(t * tq, tq), tq)]

        _, l = _flash_tiles(0, i, scores, values, (sa_sc, sb_sc), acc_sc, tq, causal)
        outs.append(acc_sc[...] * (1.0 / l))
    o_ref[...] = jnp.concatenate(outs, axis=0).T.astype(BF16)


def _mla_attention(mqt, mk, mvt, batch, seq):
    tq = MLA_TILE
    nq = seq // tq
    pairs = MLA_HEADS // 2
    return pl.pallas_call(
        functools.partial(_mla_kernel, tq=tq),
        grid=(batch, pairs, nq),
        in_specs=[pl.BlockSpec((2 * LANES, tq), lambda b, p, i: (p, b * nq + i)),
                  pl.BlockSpec((seq, 2 * LANES), lambda b, p, i: (b, p)),
                  pl.BlockSpec((2 * MLA_VDIM, seq), lambda b, p, i: (p, b))],
        out_specs=pl.BlockSpec((tq, LANES), lambda b, p, i: (b * nq + i, p)),
        out_shape=jax.ShapeDtypeStruct((batch * seq, pairs * LANES), BF16),
        scratch_shapes=[pltpu.VMEM((MLA_VDIM, tq), F32), pltpu.VMEM((tq, tq), F32), pltpu.VMEM((tq, tq), F32)],
        compiler_params=_params(("parallel", "parallel", "arbitrary")),
        name="mla_attention",
    )(mqt, mk, mvt)


def _out_ffn_kernel(x_ref, on_ref, om_ref, u_ref, uh_ref, pw_ref, ps_ref, won_ref, wom_ref, wop_ref, g2_ref,
                    w1_ref, w2_ref, gf_ref, y_ref, *, tm, seq, final):
    i = pl.program_id(0)
    t0 = (i * tm) % seq
    u = u_ref[...]
    halo = jnp.where(t0 == 0, 0.0, uh_ref[...])
    ext = jnp.concatenate([halo, u], axis=0)
    s2 = ext + pltpu.roll(ext, 1, axis=0)
    s4 = s2 + pltpu.roll(s2, 2, axis=0)
    s8 = s4 + pltpu.roll(s4, 4, axis=0)
    s16 = s8 + pltpu.roll(s8, 8, axis=0)
    sl = slice(POOL_HALO, POOL_HALO + tm)
    lane = lax.broadcasted_iota(jnp.int32, (tm, POOL_WIDTH), 1)
    t = t0 + lax.broadcasted_iota(jnp.int32, (tm, POOL_WIDTH), 0)
    sums = jnp.where(lane < 64, s2[sl], jnp.where(lane < 128, s4[sl], jnp.where(lane < 192, s8[sl], s16[sl])))
    win = jnp.where(lane < 64, 2, jnp.where(lane < 128, 4, jnp.where(lane < 192, 8, 16)))
    cnt = jnp.minimum(t + 1, win).astype(F32)
    pooled = (sums / cnt - u).astype(BF16)
    y_pool = (_dot(pooled, pw_ref[...]) * ps_ref[...]).astype(BF16)
    mix = _dot(on_ref[...], won_ref[...]) + _dot(om_ref[...], wom_ref[...]) + _dot(y_pool, wop_ref[...])
    x = x_ref[...] + mix
    h = _rms(x, g2_ref[...]).astype(BF16)
    ff = None
    for c in range(D_FF // FF_CHUNK):
        cs = slice(c * FF_CHUNK, (c + 1) * FF_CHUNK)
        a = jnp.maximum(_dot(h, w1_ref[:, cs]), 0.0)
        part = _dot((a * a).astype(BF16), w2_ref[cs, :])
        ff = part if ff is None else ff + part
    acc = x + ff
    if final:
        acc = _rms(acc, gf_ref[...])
    y_ref[...] = acc


def _out_ffn(x, o_nsa, o_mla, u, pw, ps, won, wom, wop, g2, w1, w2, gf, seq, final):
    n = x.shape[0]
    tm = ROW_TILE
    hb = tm // POOL_HALO
    row = lambda width: pl.BlockSpec((tm, width), lambda i: (i, 0))
    return pl.pallas_call(
        functools.partial(_out_ffn_kernel, tm=tm, seq=seq, final=final),
        grid=(n // tm,),
        in_specs=[row(D_MODEL), row(o_nsa.shape[1]), row(o_mla.shape[1]), row(POOL_WIDTH),
                  pl.BlockSpec((POOL_HALO, POOL_WIDTH), lambda i: (jnp.maximum(i * hb - 1, 0), 0)),
                  _const_spec(pw.shape), _const_spec(ps.shape), _const_spec(won.shape), _const_spec(wom.shape),
                  _const_spec(wop.shape), _const_spec(g2.shape), _const_spec(w1.shape), _const_spec(w2.shape),
                  _const_spec(gf.shape)],
        out_specs=row(D_MODEL),
        out_shape=jax.ShapeDtypeStruct((n, D_MODEL), F32),
        compiler_params=_params(("parallel",)),
        name="out_ffn",
    )(x, o_nsa, o_mla, u, u, pw, ps, won, wom, wop, g2, w1, w2, gf)


def _overlap_t(seq):
    ncp = seq // NSA_CMP_STRIDE
    n_slc = seq // NSA_SLC_LEN
    start = np.arange(ncp - 1)[None, :] * NSA_CMP_STRIDE
    end = start + NSA_CMP_LEN
    s0 = np.arange(n_slc)[:, None] * NSA_SLC_LEN
    s1 = s0 + NSA_SLC_LEN
    ov = np.clip(np.minimum(end, s1) - np.maximum(start, s0), 0, None) / NSA_CMP_LEN
    out = np.zeros((LANES, ncp), np.float32)
    out[:n_slc, :ncp - 1] = ov
    return jnp.asarray(out, BF16)


def _sublane_replicator(seq):
    n_slc = seq // NSA_SLC_LEN
    rep = (np.arange(n_slc * SUBLANES)[:, None] // SUBLANES == np.arange(n_slc)[None, :]).astype(np.float32)
    return jnp.asarray(rep, BF16)


def kernel(x, positions, ln1_g, w_in, nsa_cmp_pos, nsa_cmp_w1, nsa_cmp_w2, mla_q_norm, mla_w_qup, mla_kv_norm,
           mla_w_kvup, pool_w, pool_scale, w_out, ln2_g, w_ff1, w_ff2, final_g):
    batch, seq, _ = x.shape
    depth = w_in.shape[0]
    n = batch * seq
    assert n % ROW_TILE == 0 and seq % ROW_TILE == 0 and seq % NSA_TILE == 0 and seq % MLA_TILE == 0
    assert seq // NSA_SLC_LEN <= LANES and NSA_WINDOW % NSA_TILE == 0 and NSA_TILE % NSA_SLC_LEN == 0

    cos, sin, cos_t, sin_t = _rope_tables(positions)
    ovt = _overlap_t(seq)
    rep = _sublane_replicator(seq)
    in_layout, in_layout_t = _in_layout(), _in_layout_t()
    lq, lqr, lk, lv = _mla_layouts()
    won_layout = _layout([r for g in range(NSA_KV_HEADS)
                          for r in ((g * NSA_HPG * NSA_DH, NSA_HPG * NSA_DH, 1.0), (None, NSA_DH, 0.0))])
    row2 = lambda v: v.reshape(1, -1)

    xf = x.reshape(n, D_MODEL)
    for l in range(depth):
        w_all = _take_cols(w_in[l], in_layout)
        wt_all = _take_cols(w_in[l], in_layout_t).T
        wqt, wqrt = _take_cols(mla_w_qup[l], lq).T, _take_cols(mla_w_qup[l], lqr).T
        wk, wvt = _take_cols(mla_w_kvup[l], lk), _take_cols(mla_w_kvup[l], lv).T
        q6t, vst, vwt, gate_t, kvc, kvs, kvw, mqt, mk, mvt, u = _proj_in(
            xf, row2(ln1_g[l]), w_all, wt_all, cos, sin, cos_t, sin_t, row2(mla_q_norm[l]), row2(mla_kv_norm[l]),
            wqt, wqrt, wk, wvt)

        w1c = _blockdiag2(nsa_cmp_w1[l, 0], nsa_cmp_w1[l, 1]).astype(BF16)
        w2c = _blockdiag2(nsa_cmp_w2[l, 0], nsa_cmp_w2[l, 1]).astype(BF16)
        pos = jnp.concatenate([nsa_cmp_pos[l, 0], nsa_cmp_pos[l, 1]], axis=-1)[:, None, :]
        half = NSA_CMP_STRIDE
        kvcmp, vct = _compress(kvc, pos[:half], pos[half:], w1c[:half], w1c[half:], w2c, batch, seq)

        o_nsa = _nsa_attention(q6t, kvcmp, vct, kvs, vst, kvw, vwt, gate_t, ovt, rep, batch, seq)
        o_mla = _mla_attention(mqt, mk, mvt, batch, seq)

        pw = _blockdiag2(_blockdiag2(pool_w[l, 0], pool_w[l, 1]), _blockdiag2(pool_w[l, 2], pool_w[l, 3])).astype(BF16)
        won = _take_rows(w_out[l], won_layout)
        wom = w_out[l, 384:768].astype(BF16)
        wop = w_out[l, 768:1024].astype(BF16)
        xf = _out_ffn(xf, o_nsa, o_mla, u, pw, row2(pool_scale[l]), won, wom, wop, row2(ln2_g[l]),
                      w_ff1[l].astype(BF16), w_ff2[l].astype(BF16), row2(final_g), seq, final=(l == depth - 1))
    return xf.reshape(batch, seq, D_MODEL)
```

```python
import functools

import numpy as np
import jax
import jax.numpy as jnp
from jax import lax
from jax.experimental import pallas as pl
from jax.experimental.pallas import tpu as pltpu

F32 = jnp.float32
BF16 = jnp.bfloat16

D_MODEL = 1024
NSA_HEADS = 6
NSA_KV_HEADS = 2
NSA_HPG = 3
NSA_DH = 64
NSA_CMP_LEN = 32
NSA_CMP_STRIDE = 16
NSA_SLC_LEN = 64
NSA_TOPN = 16
NSA_WINDOW = 512
NSA_FORCE = 1.0e4
MLA_HEADS = 6
MLA_NOPE = 64
MLA_ROPE = 32
MLA_VDIM = 64
V_ROWS = 80
ROPE_THETA = 10000.0
POOL_WIDTH = 256
POOL_HALO = 16
D_FF = 4096
EPS = 1e-6

LANES = 128
SUBLANES = 8
VMEM_LIMIT = 56 * 1024 * 1024

MASKED = -1.0e30
M_FLOOR = -1.0e20
LOG2E = 1.4426950408889634

ROW_TILE = 512
NSA_TILE = 256
MLA_TILE = 512
FF_CHUNK = 1024


def _dot(a, b):
    return jnp.dot(a, b, preferred_element_type=F32)


def _dot_nt(a, b):
    return lax.dot_general(a, b, (((1,), (1,)), ((), ())), preferred_element_type=F32)


def _rms(x, g):
    return x * lax.rsqrt(jnp.mean(x * x, axis=-1, keepdims=True) + EPS) * g


def _layout(spec):
    idx, scl = [], []
    for start, n, s in spec:
        if start is None:
            idx.extend([0] * n)
            scl.extend([0.0] * n)
        else:
            idx.extend(range(start, start + n))
            scl.extend([s] * n)
    return np.asarray(idx, np.int32), np.asarray(scl, np.float32)


def _nkv_col(br, kvi, g):
    return 384 + br * 256 + kvi * 128 + g * 64


def _in_layout():
    spec = []
    for br in range(3):
        for g in range(2):
            spec += [(_nkv_col(br, 0, g), 64, 1.0)]
            spec += [(None, 64, 0.0)] if br == 1 else [(_nkv_col(br, 1, g), 64, 1.0)]
    spec += [(1170, 256, 1.0), (1426, 128, 1.0)]
    spec += [(None, 64, 0.0), (1554, 32, 1.0), (None, 32, 0.0)]
    spec += [(None, 64, 0.0), (1570, 16, -1.0), (1554, 16, 1.0), (None, 32, 0.0)]
    spec += [(1586, 256, 1.0)]
    return _layout(spec)


def _in_layout_t():
    spec = []
    for h in range(NSA_HEADS):
        spec += [(h * 64, 64, 0.125), (None, 64, 0.0)]
    for br in (1, 2):
        for g in range(2):
            spec += [(_nkv_col(br, 1, g), 64, 1.0), (None, V_ROWS - 64, 0.0)]
    for g in range(2):
        spec += [(1152 + g * 9, 9, 1.0), (None, LANES - 9, 0.0)]
    return _layout(spec)


def _mla_layouts():
    q, qr, k, v = [], [], [], []
    for h in range(MLA_HEADS):
        b = h * 96
        q += [(b, 96, 1.0), (None, 32, 0.0)]
        qr += [(None, 64, 0.0), (b + 80, 16, -1.0), (b + 64, 16, 1.0), (None, 32, 0.0)]
        k += [(h * 128, 64, 1.0), (None, 64, 0.0)]
        v += [(h * 128 + 64, 64, 1.0), (None, V_ROWS - 64, 0.0)]
    return _layout(q), _layout(qr), _layout(k), _layout(v)


def _take_cols(w, layout):
    idx, scl = layout
    return (jnp.take(w, idx, axis=1) * scl[None, :]).astype(BF16)


def _take_rows(w, layout):
    idx, scl = layout
    return (jnp.take(w, idx, axis=0) * scl[:, None]).astype(BF16)


def _blockdiag2(a, b):
    z = jnp.zeros_like(a)
    top = jnp.concatenate([a, z], axis=-1)
    bot = jnp.concatenate([z, b], axis=-1)
    return jnp.concatenate([top, bot], axis=-2)


def _const_spec(shape):
    nd = len(shape)
    return pl.BlockSpec(shape, lambda *_: (0,) * nd, pipeline_mode=pl.Buffered(1))


def _params(sem):
    return pltpu.CompilerParams(dimension_semantics=sem, vmem_limit_bytes=VMEM_LIMIT)


def _rope_kernel(posc_ref, posr_ref, invr_ref, invc_ref, cos_ref, sin_ref, cost_ref, sint_ref):
    ang = posc_ref[...].astype(F32) * invr_ref[...]
    cos_ref[...] = jnp.cos(ang)
    sin_ref[...] = jnp.sin(ang)
    ang_t = invc_ref[...] * posr_ref[...].astype(F32)
    cost_ref[...] = jnp.cos(ang_t)
    sint_ref[...] = jnp.sin(ang_t)


def _rope_tables(positions):
    n = positions.size
    half = MLA_ROPE // 2
    inv = jnp.power(jnp.float32(ROPE_THETA), -(jnp.arange(half, dtype=F32) / half))
    inv128 = jnp.zeros((LANES,), F32).at[64:64 + half].set(inv).at[64 + half:64 + 2 * half].set(inv)
    tm = ROW_TILE
    out = jax.ShapeDtypeStruct((n, LANES), F32)
    out_t = jax.ShapeDtypeStruct((LANES, n), F32)
    return pl.pallas_call(
        _rope_kernel,
        grid=(n // tm,),
        in_specs=[pl.BlockSpec((tm, 1), lambda i: (i, 0)), pl.BlockSpec((1, tm), lambda i: (0, i)),
                  _const_spec((1, LANES)), _const_spec((LANES, 1))],
        out_specs=[pl.BlockSpec((tm, LANES), lambda i: (i, 0))] * 2 + [pl.BlockSpec((LANES, tm), lambda i: (0, i))] * 2,
        out_shape=[out, out, out_t, out_t],
        compiler_params=_params(("parallel",)),
        name="rope_tables",
    )(positions.reshape(n, 1), positions.reshape(1, n), inv128.reshape(1, LANES), inv128.reshape(LANES, 1))


def _ones_rows(rows, cols):
    r = lax.broadcasted_iota(jnp.int32, (rows, cols), 0) % V_ROWS
    return jnp.where(r >= 64, 1.0, 0.0)


def _proj_in_kernel(x_ref, g_ref, w_ref, wt_ref, cos_ref, sin_ref, cost_ref, sint_ref, qg_ref, kvg_ref,
                    wqt_ref, wqrt_ref, wk_ref, wvt_ref,
                    q6t_ref, vst_ref, vwt_ref, gatet_ref, kvc_ref, kvs_ref, kvw_ref, mqt_ref, mk_ref, mvt_ref, u_ref,
                    *, seq):
    h = _rms(x_ref[...], g_ref[...]).astype(BF16)
    zt = _dot_nt(wt_ref[...], h)
    tm = h.shape[0]
    nv = NSA_KV_HEADS * V_ROWS
    q6t_ref[...] = (zt[0:768] * LOG2E).astype(BF16)
    vst_ref[...] = (zt[768:768 + nv] + _ones_rows(nv, tm)).astype(BF16)
    vwt_ref[...] = (zt[768 + nv:768 + 2 * nv] + _ones_rows(nv, tm)).astype(BF16)
    gatet_ref[...] = jax.nn.sigmoid(zt[768 + 2 * nv:768 + 2 * nv + 256])
    z = _dot(h, w_ref[:, 0:768])
    kvc_ref[...] = z[:, 0:256]
    tok = (pl.program_id(0) * tm) % seq + lax.broadcasted_iota(jnp.int32, (tm, 2 * LANES), 0)
    lane = lax.broadcasted_iota(jnp.int32, (tm, 2 * LANES), 1) & (LANES - 1)
    onehot = jnp.where(lane - NSA_DH == tok // NSA_SLC_LEN, 1.0, 0.0)
    kvs_ref[...] = (z[:, 256:512] + onehot).astype(BF16)
    kvw_ref[...] = z[:, 512:768].astype(BF16)
    z = _dot(h, w_ref[:, 768:1664])
    u_ref[...] = z[:, 640:896]
    cqn = _rms(z[:, 0:256], qg_ref[...]).astype(BF16)
    qa = _dot_nt(wqt_ref[...], cqn)
    qb = _dot_nt(wqrt_ref[...], cqn)
    cos_t = cost_ref[...]
    sin_t = sint_ref[...]
    scale = (MLA_NOPE + MLA_ROPE) ** -0.5 * LOG2E
    for hh in range(MLA_HEADS):
        sl = slice(hh * LANES, (hh + 1) * LANES)
        mqt_ref[sl, :] = ((qa[sl] * cos_t + qb[sl] * sin_t) * scale).astype(BF16)
    ckvn = _rms(z[:, 256:384], kvg_ref[...]).astype(BF16)
    k_rope = z[:, 384:512] * cos_ref[...] + z[:, 512:640] * sin_ref[...]
    ka = _dot(ckvn, wk_ref[...])
    for hh in range(MLA_HEADS):
        sl = slice(hh * LANES, (hh + 1) * LANES)
        mk_ref[:, sl] = (ka[:, sl] + k_rope).astype(BF16)
    mvt_ref[...] = (_dot_nt(wvt_ref[...], ckvn) + _ones_rows(MLA_HEADS * V_ROWS, tm)).astype(BF16)


def _proj_in(x, g, w_all, wt_all, cos, sin, cos_t, sin_t, qg, kvg, wqt, wqrt, wk, wvt, seq):
    n = x.shape[0]
    tm = ROW_TILE
    row = lambda width: pl.BlockSpec((tm, width), lambda i: (i, 0))
    col = lambda height: pl.BlockSpec((height, tm), lambda i: (0, i))
    nv = NSA_KV_HEADS * V_ROWS
    outs = [("t", 768, BF16), ("t", nv, BF16), ("t", nv, BF16), ("t", 256, F32), ("r", 256, F32), ("r", 256, BF16),
            ("r", 256, BF16), ("t", 768, BF16), ("r", 768, BF16), ("t", MLA_HEADS * V_ROWS, BF16), ("r", 256, F32)]
    return pl.pallas_call(
        functools.partial(_proj_in_kernel, seq=seq),
        grid=(n // tm,),
        in_specs=[row(D_MODEL), _const_spec(g.shape), _const_spec(w_all.shape), _const_spec(wt_all.shape),
                  row(LANES), row(LANES), col(LANES), col(LANES), _const_spec(qg.shape), _const_spec(kvg.shape),
                  _const_spec(wqt.shape), _const_spec(wqrt.shape), _const_spec(wk.shape), _const_spec(wvt.shape)],
        out_specs=[col(w) if kind == "t" else row(w) for kind, w, _ in outs],
        out_shape=[jax.ShapeDtypeStruct((w, n) if kind == "t" else (n, w), dt) for kind, w, dt in outs],
        compiler_params=_params(("parallel",)),
        name="proj_in",
    )(x, g, w_all, wt_all, cos, sin, cos_t, sin_t, qg, kvg, wqt, wqrt, wk, wvt)


def _compress_kernel(kvc_ref, posa_ref, posb_ref, wa_ref, wb_ref, w2_ref, o_ref, vt_ref, *, ncp):
    a = jnp.zeros((ncp, LANES), F32)
    b = jnp.zeros((ncp, LANES), F32)
    for l in range(NSA_CMP_STRIDE):
        xl = kvc_ref[pl.ds(l, ncp, stride=NSA_CMP_STRIDE), :]
        a = a + _dot((xl + posa_ref[l]).astype(BF16), wa_ref[l])
        b = b + _dot((xl + posb_ref[l]).astype(BF16), wb_ref[l])
    pre = a + pltpu.roll(b, ncp - 1, axis=0)
    hid = jax.nn.gelu(pre).astype(BF16)
    out = _dot(hid, w2_ref[...])
    o_ref[...] = out.astype(BF16)
    ones = jnp.ones((V_ROWS - NSA_DH, ncp), F32)
    vt_ref[...] = jnp.concatenate([out.T[NSA_DH:2 * NSA_DH], ones], axis=0).astype(BF16)


def _compress(kvc, posa, posb, wa, wb, w2, batch, seq):
    ncp = seq // NSA_CMP_STRIDE
    return pl.pallas_call(
        functools.partial(_compress_kernel, ncp=ncp),
        grid=(batch, NSA_KV_HEADS),
        in_specs=[pl.BlockSpec((seq, LANES), lambda b, g: (b, g)), _const_spec(posa.shape), _const_spec(posb.shape),
                  _const_spec(wa.shape), _const_spec(wb.shape), _const_spec(w2.shape)],
        out_specs=[pl.BlockSpec((ncp, LANES), lambda b, g: (b, g)),
                   pl.BlockSpec((V_ROWS, ncp), lambda b, g: (b * NSA_KV_HEADS + g, 0))],
        out_shape=[jax.ShapeDtypeStruct((batch * ncp, 2 * LANES), BF16),
                   jax.ShapeDtypeStruct((batch * NSA_KV_HEADS * V_ROWS, ncp), BF16)],
        compiler_params=_params(("parallel", "parallel")),
        name="nsa_compress",
    )(kvc, posa, posb, wa, wb, w2)


def _flash_update(buf, v_t, m, acc_ref, mask=None):
    load = (lambda: buf[...]) if mask is None else (lambda: jnp.where(mask, buf[...], MASKED))
    m_new = jnp.maximum(m, jnp.max(load(), axis=0, keepdims=True))
    alpha = jnp.exp2(m - m_new)
    p = jnp.exp2(load() - m_new)
    acc_ref[...] = alpha * acc_ref[...] + _dot(v_t, p.astype(BF16))
    return m_new


def _flash_tiles(first, last, scores, values, bufs, acc_ref, width, last_mask, first_mask=None, unroll=2):
    n = last - first
    s_first = scores(first)
    bufs[0][...] = s_first if first_mask is None else jnp.where(first_mask, s_first, MASKED)
    acc_ref[...] = jnp.zeros(acc_ref.shape, F32)

    def steps(count, t0):
        def body(jj, m):
            for k in range(count):
                t = t0 + count * jj + k
                cur = bufs[k % 2]
                if count > 1:
                    bufs[(k + 1) % 2][...] = scores(t + 1)
                m = _flash_update(cur, values(t), m, acc_ref)
                if count == 1:
                    cur[...] = scores(t + 1)
            return m
        return body

    m = jnp.full((1, width), M_FLOOR, F32)
    t0, rem, level = first, n, unroll
    while level >= 1:
        trips = rem // level
        m = lax.fori_loop(0, trips, steps(level, t0), m)
        t0, rem, level = t0 + trips * level, rem - trips * level, level // 2
    _flash_update(bufs[0], values(last), m, acc_ref, mask=last_mask)
    return acc_ref[0:64] * (1.0 / acc_ref[64:65])


def _nsa_kernel(qt_ref, kvc_ref, vct_ref, kvs_ref, vst_ref, kvw_ref, vwt_ref, gt_ref, ovt_ref, o_ref,
                bias_sc, acc_sc, sa_sc, sb_sc, *, tq, n_slc, n_sel):
    i = pl.program_id(2)
    q0 = pl.multiple_of(i * tq, tq)
    width = NSA_HPG * tq
    q3 = jnp.concatenate([qt_ref[h * LANES:(h + 1) * LANES, :] for h in range(NSA_HPG)], axis=1)

    kvc = kvc_ref[...]
    ncp = kvc.shape[0]
    s = _dot(kvc, q3)
    nblk = lax.broadcasted_iota(jnp.int32, (ncp, width), 0)
    qpos = q0 + (lax.broadcasted_iota(jnp.int32, (1, width), 1) & (tq - 1))
    last_blk = (qpos - (NSA_CMP_LEN - 1)) >> 4
    s = jnp.where(nblk <= last_blk, s, -jnp.inf)
    m = jnp.max(s, axis=0, keepdims=True)
    m = jnp.where(m == -jnp.inf, 0.0, m)
    e = jnp.exp2(s - m)
    d = jnp.sum(e, axis=0, keepdims=True)
    p = e * (1.0 / jnp.where(d > 0.0, d, 1.0))
    o_cmp = _dot(vct_ref[0:NSA_DH, :], p.astype(BF16))

    every_valid_block_selected = q0 + tq <= n_sel * NSA_SLC_LEN

    @pl.when(every_valid_block_selected)
    def _():
        bias_sc[...] = jnp.zeros(bias_sc.shape, BF16)

    @pl.when(jnp.logical_not(every_valid_block_selected))
    def _():
        psum = p[:, 0:tq] + p[:, tq:2 * tq] + p[:, 2 * tq:3 * tq]
        hi = psum.astype(BF16)
        lo = (psum - hi.astype(F32)).astype(BF16)
        ovt = ovt_ref[...]
        imp = _dot(ovt, hi) + _dot(ovt, lo)
        blk = lax.broadcasted_iota(jnp.int32, (LANES, tq), 0)
        qp = q0 + lax.broadcasted_iota(jnp.int32, (LANES, tq), 1)
        cur = qp >> 6
        forced = (blk == 0) | (blk == cur) | (blk == cur - 1)
        valid = (blk << 6) <= qp
        score = jnp.where(forced, NSA_FORCE, jnp.where(valid, imp, -1.0))[0:n_slc]
        sub = SUBLANES
        sidx = lax.broadcasted_iota(jnp.int32, (sub, tq), 0)
        groups = [score[k * sub:(k + 1) * sub] for k in range(n_slc // sub)]
        cnts = [jnp.zeros((sub, tq), F32) for _ in groups]
        for sp in range(n_slc):
            col = score[sp:sp + 1, :]
            for k, grp in enumerate(groups):
                if k * sub > sp:
                    cnts[k] = cnts[k] + jnp.where(col >= grp, 1.0, 0.0)
                elif (k + 1) * sub - 1 <= sp:
                    cnts[k] = cnts[k] + jnp.where(col > grp, 1.0, 0.0)
                else:
                    after = jnp.where(col >= grp, 1.0, 0.0)
                    cnts[k] = cnts[k] + jnp.where(sidx + k * sub > sp, after, jnp.where(col > grp, 1.0, 0.0))
        bias = jnp.where(jnp.concatenate(cnts, axis=0) < n_sel, 0.0, MASKED)
        if n_slc < NSA_DH:
            bias = jnp.concatenate([bias, jnp.zeros((NSA_DH - n_slc, tq), F32)], axis=0)
        bias_sc[...] = bias.astype(BF16)

    cc = lax.broadcasted_iota(jnp.int32, (tq, width), 0)
    rr = lax.broadcasted_iota(jnp.int32, (tq, width), 1) & (tq - 1)

    bias = bias_sc[...]
    q3_slc = jnp.concatenate([q3[0:NSA_DH], jnp.concatenate([bias] * NSA_HPG, axis=1)], axis=0)

    def slc_scores(t):
        return _dot(kvs_ref[pl.ds(pl.multiple_of(t * tq, tq), tq), :], q3_slc)

    def slc_values(t):
        return vst_ref[:, pl.ds(pl.multiple_of(t * tq, tq), tq)]

    o_slc = _flash_tiles(0, i, slc_scores, slc_values, (sa_sc, sb_sc), acc_sc, width, cc <= rr, unroll=4)

    def win_scores(t):
        return _dot(kvw_ref[pl.ds(pl.multiple_of(t * tq, tq), tq), :], q3)

    def win_values(t):
        return vwt_ref[:, pl.ds(pl.multiple_of(t * tq, tq), tq)]

    back = NSA_WINDOW // tq
    o_win = _flash_tiles(jnp.maximum(i - back, 0), i, win_scores, win_values, (sa_sc, sb_sc), acc_sc, width,
                         cc <= rr, first_mask=jnp.logical_or(cc > rr, i < back))

    gt = gt_ref[...]
    outs = []
    for h in range(NSA_HPG):
        sl = slice(h * tq, (h + 1) * tq)
        outs.append(gt[3 * h:3 * h + 1] * o_cmp[:, sl] + gt[3 * h + 1:3 * h + 2] * o_slc[:, sl]
                    + gt[3 * h + 2:3 * h + 3] * o_win[:, sl])
    outs.append(jnp.zeros((NSA_DH, tq), F32))
    o_ref[...] = jnp.concatenate(outs, axis=0).T.astype(BF16)


def _nsa_attention(q6t, kvcmp, vct, kvs, vst, kvw, vwt, gate_t, ovt, batch, seq):
    tq = NSA_TILE
    nq = seq // tq
    ncp = seq // NSA_CMP_STRIDE
    n_slc = seq // NSA_SLC_LEN
    width = NSA_HPG * tq
    kern = functools.partial(_nsa_kernel, tq=tq, n_slc=n_slc, n_sel=min(NSA_TOPN, n_slc))
    kv_spec = pl.BlockSpec((seq, LANES), lambda b, g, i: (b, g))
    vt_spec = pl.BlockSpec((V_ROWS, seq), lambda b, g, i: (g, b))
    return pl.pallas_call(
        kern,
        grid=(batch, NSA_KV_HEADS, nq),
        in_specs=[pl.BlockSpec((NSA_HPG * LANES, tq), lambda b, g, i: (g, b * nq + i)),
                  pl.BlockSpec((ncp, LANES), lambda b, g, i: (b, g)),
                  pl.BlockSpec((V_ROWS, ncp), lambda b, g, i: (b * NSA_KV_HEADS + g, 0)),
                  kv_spec, vt_spec, kv_spec, vt_spec,
                  pl.BlockSpec((LANES, tq), lambda b, g, i: (g, b * nq + i)),
                  _const_spec(ovt.shape)],
        out_specs=pl.BlockSpec((tq, 2 * LANES), lambda b, g, i: (b * nq + i, g)),
        out_shape=jax.ShapeDtypeStruct((batch * seq, NSA_KV_HEADS * 2 * LANES), BF16),
        scratch_shapes=[pltpu.VMEM((NSA_DH, tq), BF16), pltpu.VMEM((V_ROWS, width), F32),
                        pltpu.VMEM((tq, width), F32), pltpu.VMEM((tq, width), F32)],
        compiler_params=_params(("parallel", "parallel", "arbitrary")),
        name="nsa_attention",
    )(q6t, kvcmp, vct, kvs, vst, kvw, vwt, gate_t, ovt)


def _mla_kernel(qt_ref, k_ref, vt_ref, o_ref, acc_sc, sa_sc, sb_sc, *, tq):
    i = pl.program_id(2)
    causal = lax.broadcasted_iota(jnp.int32, (tq, tq), 0) <= lax.broadcasted_iota(jnp.int32, (tq, tq), 1)
    outs = []
    for hh in range(2):
        def scores(t, hh=hh):
            off = pl.multiple_of(t * tq, tq)
            return _dot(k_ref[pl.ds(off, tq), hh * LANES:(hh + 1) * LANES], qt_ref[hh * LANES:(hh + 1) * LANES, :])

        def values(t, hh=hh):
            return vt_ref[hh * V_ROWS:(hh + 1) * V_ROWS, pl.ds(pl.multiple_of(t * tq, tq), tq)]

        outs.append(_flash_tiles(0, i, scores, values, (sa_sc, sb_sc), acc_sc, tq, causal, unroll=4))
    o_ref[...] = jnp.concatenate(outs, axis=0).T.astype(BF16)


def _mla_attention(mqt, mk, mvt, batch, seq):
    tq = MLA_TILE
    nq = seq // tq
    pairs = MLA_HEADS // 2
    return pl.pallas_call(
        functools.partial(_mla_kernel, tq=tq),
        grid=(batch, pairs, nq),
        in_specs=[pl.BlockSpec((2 * LANES, tq), lambda b, p, i: (p, b * nq + i)),
                  pl.BlockSpec((seq, 2 * LANES), lambda b, p, i: (b, p)),
                  pl.BlockSpec((2 * V_ROWS, seq), lambda b, p, i: (p, b))],
        out_specs=pl.BlockSpec((tq, LANES), lambda b, p, i: (b * nq + i, p)),
        out_shape=jax.ShapeDtypeStruct((batch * seq, pairs * LANES), BF16),
        scratch_shapes=[pltpu.VMEM((V_ROWS, tq), F32), pltpu.VMEM((tq, tq), F32), pltpu.VMEM((tq, tq), F32)],
        compiler_params=_params(("parallel", "parallel", "arbitrary")),
        name="mla_attention",
    )(mqt, mk, mvt)


def _out_ffn_kernel(x_ref, on_ref, om_ref, u_ref, uh_ref, pw_ref, ps_ref, won_ref, wom_ref, wop_ref, g2_ref,
                    w1_ref, w2_ref, gf_ref, y_ref, *, tm, seq, final):
    i = pl.program_id(0)
    t0 = (i * tm) % seq
    u = u_ref[...]
    halo = jnp.where(t0 == 0, 0.0, uh_ref[...])
    ext = jnp.concatenate([halo, u], axis=0)
    s2 = ext + pltpu.roll(ext, 1, axis=0)
    s4 = s2 + pltpu.roll(s2, 2, axis=0)
    s8 = s4 + pltpu.roll(s4, 4, axis=0)
    s16 = s8 + pltpu.roll(s8, 8, axis=0)
    sl = slice(POOL_HALO, POOL_HALO + tm)
    lane = lax.broadcasted_iota(jnp.int32, (tm, POOL_WIDTH), 1)
    t = t0 + lax.broadcasted_iota(jnp.int32, (tm, POOL_WIDTH), 0)
    sums = jnp.where(lane < 64, s2[sl], jnp.where(lane < 128, s4[sl], jnp.where(lane < 192, s8[sl], s16[sl])))
    win = jnp.where(lane < 64, 2, jnp.where(lane < 128, 4, jnp.where(lane < 192, 8, 16)))
    cnt = jnp.minimum(t + 1, win).astype(F32)
    pooled = (sums / cnt - u).astype(BF16)
    y_pool = (_dot(pooled, pw_ref[...]) * ps_ref[...]).astype(BF16)
    mix = _dot(on_ref[...], won_ref[...]) + _dot(om_ref[...], wom_ref[...]) + _dot(y_pool, wop_ref[...])
    x = x_ref[...] + mix
    h = _rms(x, g2_ref[...]).astype(BF16)
    ff = None
    for c in range(D_FF // FF_CHUNK):
        cs = slice(c * FF_CHUNK, (c + 1) * FF_CHUNK)
        a = jnp.maximum(_dot(h, w1_ref[:, cs]), 0.0)
        part = _dot((a * a).astype(BF16), w2_ref[cs, :])
        ff = part if ff is None else ff + part
    acc = x + ff
    if final:
        acc = _rms(acc, gf_ref[...])
    y_ref[...] = acc


def _out_ffn(x, o_nsa, o_mla, u, pw, ps, won, wom, wop, g2, w1, w2, gf, seq, final):
    n = x.shape[0]
    tm = ROW_TILE
    hb = tm // POOL_HALO
    row = lambda width: pl.BlockSpec((tm, width), lambda i: (i, 0))
    return pl.pallas_call(
        functools.partial(_out_ffn_kernel, tm=tm, seq=seq, final=final),
        grid=(n // tm,),
        in_specs=[row(D_MODEL), row(o_nsa.shape[1]), row(o_mla.shape[1]), row(POOL_WIDTH),
                  pl.BlockSpec((POOL_HALO, POOL_WIDTH), lambda i: (jnp.maximum(i * hb - 1, 0), 0)),
                  _const_spec(pw.shape), _const_spec(ps.shape), _const_spec(won.shape), _const_spec(wom.shape),
                  _const_spec(wop.shape), _const_spec(g2.shape), _const_spec(w1.shape), _const_spec(w2.shape),
                  _const_spec(gf.shape)],
        out_specs=row(D_MODEL),
        out_shape=jax.ShapeDtypeStruct((n, D_MODEL), F32),
        compiler_params=_params(("parallel",)),
        name="out_ffn",
    )(x, o_nsa, o_mla, u, u, pw, ps, won, wom, wop, g2, w1, w2, gf)


def _overlap_t(seq):
    ncp = seq // NSA_CMP_STRIDE
    n_slc = seq // NSA_SLC_LEN
    start = np.arange(ncp - 1)[None, :] * NSA_CMP_STRIDE
    end = start + NSA_CMP_LEN
    s0 = np.arange(n_slc)[:, None] * NSA_SLC_LEN
    s1 = s0 + NSA_SLC_LEN
    ov = np.clip(np.minimum(end, s1) - np.maximum(start, s0), 0, None) / NSA_CMP_LEN
    out = np.zeros((LANES, ncp), np.float32)
    out[:n_slc, :ncp - 1] = ov
    return jnp.asarray(out, BF16)


def kernel(x, positions, ln1_g, w_in, nsa_cmp_pos, nsa_cmp_w1, nsa_cmp_w2, mla_q_norm, mla_w_qup, mla_kv_norm,
           mla_w_kvup, pool_w, pool_scale, w_out, ln2_g, w_ff1, w_ff2, final_g):
    batch, seq, _ = x.shape
    depth = w_in.shape[0]
    n = batch * seq
    assert n % ROW_TILE == 0 and seq % ROW_TILE == 0 and seq % NSA_TILE == 0 and seq % MLA_TILE == 0
    assert seq // NSA_SLC_LEN <= NSA_DH and NSA_WINDOW % NSA_TILE == 0 and NSA_TILE % NSA_SLC_LEN == 0

    cos, sin, cos_t, sin_t = _rope_tables(positions)
    ovt = _overlap_t(seq)
    in_layout, in_layout_t = _in_layout(), _in_layout_t()
    lq, lqr, lk, lv = _mla_layouts()
    won_layout = _layout([r for g in range(NSA_KV_HEADS)
                          for r in ((g * NSA_HPG * NSA_DH, NSA_HPG * NSA_DH, 1.0), (None, NSA_DH, 0.0))])
    row2 = lambda v: v.reshape(1, -1)

    xf = x.reshape(n, D_MODEL)
    for l in range(depth):
        w_all = _take_cols(w_in[l], in_layout)
        wt_all = _take_cols(w_in[l], in_layout_t).T
        wqt, wqrt = _take_cols(mla_w_qup[l], lq).T, _take_cols(mla_w_qup[l], lqr).T
        wk, wvt = _take_cols(mla_w_kvup[l], lk), _take_cols(mla_w_kvup[l], lv).T
        q6t, vst, vwt, gate_t, kvc, kvs, kvw, mqt, mk, mvt, u = _proj_in(
            xf, row2(ln1_g[l]), w_all, wt_all, cos, sin, cos_t, sin_t, row2(mla_q_norm[l]), row2(mla_kv_norm[l]),
            wqt, wqrt, wk, wvt, seq)

        w1c = _blockdiag2(nsa_cmp_w1[l, 0], nsa_cmp_w1[l, 1]).astype(BF16)
        w2c = _blockdiag2(nsa_cmp_w2[l, 0], nsa_cmp_w2[l, 1]).astype(BF16)
        pos = jnp.concatenate([nsa_cmp_pos[l, 0], nsa_cmp_pos[l, 1]], axis=-1)[:, None, :]
        half = NSA_CMP_STRIDE
        kvcmp, vct = _compress(kvc, pos[:half], pos[half:], w1c[:half], w1c[half:], w2c, batch, seq)

        o_nsa = _nsa_attention(q6t, kvcmp, vct, kvs, vst, kvw, vwt, gate_t, ovt, batch, seq)
        o_mla = _mla_attention(mqt, mk, mvt, batch, seq)

        pw = _blockdiag2(_blockdiag2(pool_w[l, 0], pool_w[l, 1]), _blockdiag2(pool_w[l, 2], pool_w[l, 3])).astype(BF16)
        won = _take_rows(w_out[l], won_layout)
        wom = w_out[l, 384:768].astype(BF16)
        wop = w_out[l, 768:1024].astype(BF16)
        xf = _out_ffn(xf, o_nsa, o_mla, u, pw, row2(pool_scale[l]), won, wom, wop, row2(ln2_g[l]),
                      w_ff1[l].astype(BF16), w_ff2[l].astype(BF16), row2(final_g), seq, final=(l == depth - 1))
    return xf.reshape(batch, seq, D_MODEL)
```

```python
import functools

import numpy as np
import jax
import jax.numpy as jnp
from jax import lax
from jax.experimental import pallas as pl
from jax.experimental.pallas import tpu as pltpu

F32 = jnp.float32
BF16 = jnp.bfloat16

D_MODEL = 1024
NSA_HEADS = 6
NSA_KV_HEADS = 2
NSA_HPG = 3
NSA_DH = 64
NSA_CMP_LEN = 32
NSA_CMP_STRIDE = 16
NSA_SLC_LEN = 64
NSA_TOPN = 16
NSA_WINDOW = 512
NSA_FORCE = 1.0e4
MLA_HEADS = 6
MLA_NOPE = 64
MLA_ROPE = 32
MLA_VDIM = 64
V_ROWS = 80
ROPE_THETA = 10000.0
POOL_WIDTH = 256
POOL_HALO = 16
D_FF = 4096
EPS = 1e-6

LANES = 128
SUBLANES = 8
VMEM_LIMIT = 56 * 1024 * 1024

MASKED = -1.0e30
M_FLOOR = -1.0e20
LOG2E = 1.4426950408889634

ROW_TILE = 512
ROPE_TILE = 4096
NSA_TILE = 256
MLA_TILE = 512
FF_CHUNK = 1024
SELECT_VARIANTS = 4


def _dot(a, b):
    return jnp.dot(a, b, preferred_element_type=F32)


def _dot_nt(a, b):
    return lax.dot_general(a, b, (((1,), (1,)), ((), ())), preferred_element_type=F32)


def _rms(x, g):
    return x * lax.rsqrt(jnp.mean(x * x, axis=-1, keepdims=True) + EPS) * g


def _layout(spec):
    idx, scl = [], []
    for start, n, s in spec:
        if start is None:
            idx.extend([0] * n)
            scl.extend([0.0] * n)
        else:
            idx.extend(range(start, start + n))
            scl.extend([s] * n)
    return np.asarray(idx, np.int32), np.asarray(scl, np.float32)


def _nkv_col(br, kvi, g):
    return 384 + br * 256 + kvi * 128 + g * 64


def _in_layout():
    spec = []
    for br in range(3):
        for g in range(2):
            spec += [(_nkv_col(br, 0, g), 64, 1.0)]
            spec += [(None, 64, 0.0)] if br == 1 else [(_nkv_col(br, 1, g), 64, 1.0)]
    spec += [(1170, 256, 1.0), (1426, 128, 1.0)]
    spec += [(1586, 256, 1.0)]
    return _layout(spec)


def _in_layout_t():
    spec = []
    spec += [(0, NSA_HEADS * NSA_DH, 0.125)]
    for br in (1, 2):
        for g in range(2):
            spec += [(_nkv_col(br, 1, g), 64, 1.0), (None, V_ROWS - 64, 0.0)]
    for g in range(2):
        spec += [(1152 + g * 9, 9, 1.0), (None, LANES - 9, 0.0)]
    spec += [(None, 64, 0.0), (1554, 32, 1.0), (None, 32, 0.0)]
    spec += [(None, 64, 0.0), (1570, 16, -1.0), (1554, 16, 1.0), (None, 32, 0.0)]
    return _layout(spec)


def _mla_layouts():
    q, qr, k, v = [], [], [], []
    for h in range(MLA_HEADS):
        b = h * 96
        q += [(b, 96, 1.0), (None, 32, 0.0)]
        qr += [(None, 64, 0.0), (b + 80, 16, -1.0), (b + 64, 16, 1.0), (None, 32, 0.0)]
        k += [(h * 128, 64, 1.0), (None, 64, 0.0)]
        v += [(h * 128 + 64, 64, 1.0), (None, V_ROWS - 64, 0.0)]
    return _layout(q), _layout(qr), _layout(k), _layout(v)


def _take_cols(w, layout):
    idx, scl = layout
    return (jnp.take(w, idx, axis=1) * scl[None, :]).astype(BF16)


def _take_rows(w, layout):
    idx, scl = layout
    return (jnp.take(w, idx, axis=0) * scl[:, None]).astype(BF16)


def _blockdiag2(a, b):
    z = jnp.zeros_like(a)
    top = jnp.concatenate([a, z], axis=-1)
    bot = jnp.concatenate([z, b], axis=-1)
    return jnp.concatenate([top, bot], axis=-2)


def _const_spec(shape):
    nd = len(shape)
    return pl.BlockSpec(shape, lambda *_: (0,) * nd, pipeline_mode=pl.Buffered(1))


def _params(sem):
    return pltpu.CompilerParams(dimension_semantics=sem, vmem_limit_bytes=VMEM_LIMIT)


def _rope_kernel(pos_ref, inv_ref, cos_ref, sin_ref):
    ang = inv_ref[...] * pos_ref[...].astype(F32)
    cos_ref[...] = jnp.cos(ang)
    sin_ref[...] = jnp.sin(ang)


def _rope_tables(positions):
    n = positions.size
    half = MLA_ROPE // 2
    inv = jnp.power(jnp.float32(ROPE_THETA), -(jnp.arange(half, dtype=F32) / half))
    tm = min(ROPE_TILE, n)
    out = jax.ShapeDtypeStruct((half, n), F32)
    return pl.pallas_call(
        _rope_kernel,
        grid=(n // tm,),
        in_specs=[pl.BlockSpec((1, tm), lambda i: (0, i)), _const_spec((half, 1))],
        out_specs=[pl.BlockSpec((half, tm), lambda i: (0, i))] * 2,
        out_shape=[out, out],
        compiler_params=_params(("parallel",)),
        name="rope_tables",
    )(positions.reshape(1, n), inv.reshape(half, 1))


def _ones_rows(rows, cols):
    r = lax.broadcasted_iota(jnp.int32, (rows, cols), 0) % V_ROWS
    return jnp.where(r >= 64, 1.0, 0.0)


def _proj_in_kernel(x_ref, g_ref, w_ref, wt_ref, cost_ref, sint_ref, qg_ref, kvg_ref,
                    wqt_ref, wqrt_ref, wk_ref, wvt_ref,
                    q6t_ref, vst_ref, vwt_ref, gatet_ref, kvc_ref, kvs_ref, kvw_ref, mqt_ref, mk_ref, mvt_ref, u_ref,
                    *, seq):
    h = _rms(x_ref[...], g_ref[...]).astype(BF16)
    zt = _dot_nt(wt_ref[...], h)
    tm = h.shape[0]
    nv = NSA_KV_HEADS * V_ROWS
    nq = NSA_HEADS * NSA_DH
    q6t_ref[...] = (zt[0:nq] * LOG2E).astype(BF16)
    vst_ref[...] = (zt[nq:nq + nv] + _ones_rows(nv, tm)).astype(BF16)
    vwt_ref[...] = (zt[nq + nv:nq + 2 * nv] + _ones_rows(nv, tm)).astype(BF16)
    gatet_ref[...] = jax.nn.sigmoid(zt[nq + 2 * nv:nq + 2 * nv + 256])
    kr_t = zt[nq + 2 * nv + 256:nq + 2 * nv + 384]
    krrot_t = zt[nq + 2 * nv + 384:nq + 2 * nv + 512]
    z = _dot(h, w_ref[:, 0:768])
    kvc_ref[...] = z[:, 0:256]
    tok = (pl.program_id(0) * tm) % seq + lax.broadcasted_iota(jnp.int32, (tm, 2 * LANES), 0)
    lane = lax.broadcasted_iota(jnp.int32, (tm, 2 * LANES), 1) & (LANES - 1)
    onehot = jnp.where(lane - NSA_DH == tok // NSA_SLC_LEN, 1.0, 0.0)
    kvs_ref[...] = (z[:, 256:512] + onehot).astype(BF16)
    kvw_ref[...] = z[:, 512:768].astype(BF16)
    z = _dot(h, w_ref[:, 768:1408])
    u_ref[...] = z[:, 384:640]
    cqn = _rms(z[:, 0:256], qg_ref[...]).astype(BF16)
    qa = _dot_nt(wqt_ref[...], cqn)
    qb = _dot_nt(wqrt_ref[...], cqn)
    c16, s16 = cost_ref[...], sint_ref[...]
    cos_t = jnp.concatenate([jnp.ones((64, tm), F32), c16, c16, jnp.ones((32, tm), F32)], axis=0)
    sin_t = jnp.concatenate([jnp.zeros((64, tm), F32), s16, s16, jnp.zeros((32, tm), F32)], axis=0)
    scale = (MLA_NOPE + MLA_ROPE) ** -0.5 * LOG2E
    for hh in range(MLA_HEADS):
        sl = slice(hh * LANES, (hh + 1) * LANES)
        mqt_ref[sl, :] = ((qa[sl] * cos_t + qb[sl] * sin_t) * scale).astype(BF16)
    ckvn = _rms(z[:, 256:384], kvg_ref[...]).astype(BF16)
    k_rope = (kr_t * cos_t + krrot_t * sin_t).T
    ka = _dot(ckvn, wk_ref[...])
    for hh in range(MLA_HEADS):
        sl = slice(hh * LANES, (hh + 1) * LANES)
        mk_ref[:, sl] = (ka[:, sl] + k_rope).astype(BF16)
    mvt_ref[...] = (_dot_nt(wvt_ref[...], ckvn) + _ones_rows(MLA_HEADS * V_ROWS, tm)).astype(BF16)


def _proj_in(x, g, w_all, wt_all, cos_t, sin_t, qg, kvg, wqt, wqrt, wk, wvt, seq):
    n = x.shape[0]
    tm = ROW_TILE
    row = lambda width: pl.BlockSpec((tm, width), lambda i: (i, 0))
    col = lambda height: pl.BlockSpec((height, tm), lambda i: (0, i))
    nv = NSA_KV_HEADS * V_ROWS
    outs = [("t", NSA_HEADS * NSA_DH, BF16), ("t", nv, BF16), ("t", nv, BF16), ("t", 256, F32), ("r", 256, F32),
            ("r", 256, BF16), ("r", 256, BF16), ("t", 768, BF16), ("r", 768, BF16), ("t", MLA_HEADS * V_ROWS, BF16),
            ("r", 256, F32)]
    return pl.pallas_call(
        functools.partial(_proj_in_kernel, seq=seq),
        grid=(n // tm,),
        in_specs=[row(D_MODEL), _const_spec(g.shape), _const_spec(w_all.shape), _const_spec(wt_all.shape),
                  col(MLA_ROPE // 2), col(MLA_ROPE // 2), _const_spec(qg.shape), _const_spec(kvg.shape),
                  _const_spec(wqt.shape), _const_spec(wqrt.shape), _const_spec(wk.shape), _const_spec(wvt.shape)],
        out_specs=[col(w) if kind == "t" else row(w) for kind, w, _ in outs],
        out_shape=[jax.ShapeDtypeStruct((w, n) if kind == "t" else (n, w), dt) for kind, w, dt in outs],
        compiler_params=_params(("parallel",)),
        name="proj_in",
    )(x, g, w_all, wt_all, cos_t, sin_t, qg, kvg, wqt, wqrt, wk, wvt)


def _compress_kernel(kvc_ref, posa_ref, posb_ref, wa_ref, wb_ref, w2_ref, o_ref, vt_ref, *, ncp):
    a = jnp.zeros((ncp, LANES), F32)
    b = jnp.zeros((ncp, LANES), F32)
    for l in range(NSA_CMP_STRIDE):
        xl = kvc_ref[pl.ds(l, ncp, stride=NSA_CMP_STRIDE), :]
        a = a + _dot((xl + posa_ref[l]).astype(BF16), wa_ref[l])
        b = b + _dot((xl + posb_ref[l]).astype(BF16), wb_ref[l])
    pre = a + pltpu.roll(b, ncp - 1, axis=0)
    hid = jax.nn.gelu(pre).astype(BF16)
    out = _dot(hid, w2_ref[...])
    o_ref[...] = out.astype(BF16)
    ones = jnp.ones((V_ROWS - NSA_DH, ncp), F32)
    vt_ref[...] = jnp.concatenate([out.T[NSA_DH:2 * NSA_DH], ones], axis=0).astype(BF16)


def _compress(kvc, posa, posb, wa, wb, w2, batch, seq):
    ncp = seq // NSA_CMP_STRIDE
    return pl.pallas_call(
        functools.partial(_compress_kernel, ncp=ncp),
        grid=(batch, NSA_KV_HEADS),
        in_specs=[pl.BlockSpec((seq, LANES), lambda b, g: (b, g)), _const_spec(posa.shape), _const_spec(posb.shape),
                  _const_spec(wa.shape), _const_spec(wb.shape), _const_spec(w2.shape)],
        out_specs=[pl.BlockSpec((ncp, LANES), lambda b, g: (b, g)),
                   pl.BlockSpec((V_ROWS, ncp), lambda b, g: (b * NSA_KV_HEADS + g, 0))],
        out_shape=[jax.ShapeDtypeStruct((batch * ncp, 2 * LANES), BF16),
                   jax.ShapeDtypeStruct((batch * NSA_KV_HEADS * V_ROWS, ncp), BF16)],
        compiler_params=_params(("parallel", "parallel")),
        name="nsa_compress",
    )(kvc, posa, posb, wa, wb, w2)


def _flash_update(buf, v_t, m, acc_ref, mask=None):
    load = (lambda: buf[...]) if mask is None else (lambda: jnp.where(mask, buf[...], MASKED))
    m_new = jnp.maximum(m, jnp.max(load(), axis=0, keepdims=True))
    alpha = jnp.exp2(m - m_new)
    p = jnp.exp2(load() - m_new)
    acc_ref[...] = alpha * acc_ref[...] + _dot(v_t, p.astype(BF16))
    return m_new


def _flash_tiles(first, last, scores, values, bufs, acc_ref, width, last_mask, first_mask=None, unroll=2):
    n = last - first
    s_first = scores(first)
    bufs[0][...] = s_first if first_mask is None else jnp.where(first_mask, s_first, MASKED)
    acc_ref[...] = jnp.zeros(acc_ref.shape, F32)

    def steps(count, t0):
        def body(jj, m):
            for k in range(count):
                t = t0 + count * jj + k
                cur = bufs[k % 2]
                if count > 1:
                    bufs[(k + 1) % 2][...] = scores(t + 1)
                m = _flash_update(cur, values(t), m, acc_ref)
                if count == 1:
                    cur[...] = scores(t + 1)
            return m
        return body

    m = jnp.full((1, width), M_FLOOR, F32)
    t0, rem, level = first, n, unroll
    while level >= 1:
        trips = rem // level
        m = lax.fori_loop(0, trips, steps(level, t0), m)
        t0, rem, level = t0 + trips * level, rem - trips * level, level // 2
    _flash_update(bufs[0], values(last), m, acc_ref, mask=last_mask)
    return acc_ref[0:64] * (1.0 / acc_ref[64:65])


def _nsa_select(q3, q0, kvc_ref, vct_ref, ovt_ref, ocmp_ref, bias_ref, *, rows, ngroups, tq, n_slc, n_sel):
    width = NSA_HPG * tq
    ncp = kvc_ref.shape[0]
    s = _dot(kvc_ref[0:rows, :], q3)
    nblk = lax.broadcasted_iota(jnp.int32, (rows, width), 0)
    qpos = q0 + (lax.broadcasted_iota(jnp.int32, (1, width), 1) & (tq - 1))
    last_blk = (qpos - (NSA_CMP_LEN - 1)) >> 4
    s = jnp.where(nblk <= last_blk, s, -jnp.inf)
    m = jnp.max(s, axis=0, keepdims=True)
    m = jnp.where(m == -jnp.inf, 0.0, m)
    e = jnp.exp2(s - m)
    d = jnp.sum(e, axis=0, keepdims=True)
    p = e * (1.0 / jnp.where(d > 0.0, d, 1.0))

    def pad(a):
        return a if rows == ncp else jnp.concatenate([a, jnp.zeros((ncp - rows, a.shape[1]), a.dtype)], axis=0)

    ocmp_ref[...] = _dot(vct_ref[0:NSA_DH, :], pad(p.astype(BF16)))

    every_valid_block_selected = q0 + tq <= n_sel * NSA_SLC_LEN

    @pl.when(every_valid_block_selected)
    def _():
        bias_ref[...] = jnp.zeros(bias_ref.shape, BF16)

    @pl.when(jnp.logical_not(every_valid_block_selected))
    def _():
        psum = p[:, 0:tq] + p[:, tq:2 * tq] + p[:, 2 * tq:3 * tq]
        hi = psum.astype(BF16)
        lo = (psum - hi.astype(F32)).astype(BF16)
        ovt = ovt_ref[...]
        imp = _dot(ovt, pad(hi)) + _dot(ovt, pad(lo))
        nblocks = ngroups * SUBLANES
        blk = lax.broadcasted_iota(jnp.int32, (LANES, tq), 0)
        qp = q0 + lax.broadcasted_iota(jnp.int32, (LANES, tq), 1)
        cur = qp >> 6
        forced = (blk == 0) | (blk == cur) | (blk == cur - 1)
        valid = (blk << 6) <= qp
        score = jnp.where(forced, NSA_FORCE, jnp.where(valid, imp, -1.0))[0:nblocks]
        sub = SUBLANES
        sidx = lax.broadcasted_iota(jnp.int32, (sub, tq), 0)
        groups = [score[k * sub:(k + 1) * sub] for k in range(ngroups)]
        cnts = [jnp.zeros((sub, tq), F32) for _ in groups]
        for sp in range(nblocks):
            col = score[sp:sp + 1, :]
            for k, grp in enumerate(groups):
                if k * sub > sp:
                    cnts[k] = cnts[k] + jnp.where(col >= grp, 1.0, 0.0)
                elif (k + 1) * sub - 1 <= sp:
                    cnts[k] = cnts[k] + jnp.where(col > grp, 1.0, 0.0)
                else:
                    after = jnp.where(col >= grp, 1.0, 0.0)
                    cnts[k] = cnts[k] + jnp.where(sidx + k * sub > sp, after, jnp.where(col > grp, 1.0, 0.0))
        bias = jnp.where(jnp.concatenate(cnts, axis=0) < n_sel, 0.0, MASKED)
        if nblocks < NSA_DH:
            bias = jnp.concatenate([bias, jnp.zeros((NSA_DH - nblocks, tq), F32)], axis=0)
        bias_ref[...] = bias.astype(BF16)


def _nsa_kernel(qt_ref, kvc_ref, vct_ref, kvs_ref, vst_ref, kvw_ref, vwt_ref, gt_ref, ovt_ref, o_ref,
                bias_sc, ocmp_sc, acc_sc, sa_sc, sb_sc, *, tq, n_slc, n_sel):
    i = pl.program_id(2)
    q0 = pl.multiple_of(i * tq, tq)
    width = NSA_HPG * tq
    q_top = jnp.concatenate([qt_ref[h * NSA_DH:(h + 1) * NSA_DH, :] for h in range(NSA_HPG)], axis=1)
    q3 = jnp.concatenate([q_top, jnp.zeros((NSA_DH, width), BF16)], axis=0)

    ncp = kvc_ref.shape[0]
    nq = ncp * NSA_CMP_STRIDE // tq
    divisible = all(v % SELECT_VARIANTS == 0 for v in (nq, ncp // 16, n_slc // SUBLANES))
    nvar = SELECT_VARIANTS if divisible else 1
    for c in range(nvar):
        @pl.when((i * nvar) // nq == c)
        def _(c=c):
            _nsa_select(q3, q0, kvc_ref, vct_ref, ovt_ref, ocmp_sc, bias_sc, rows=ncp * (c + 1) // nvar,
                        ngroups=(n_slc // SUBLANES) * (c + 1) // nvar, tq=tq, n_slc=n_slc, n_sel=n_sel)
    o_cmp = ocmp_sc[...]

    cc = lax.broadcasted_iota(jnp.int32, (tq, width), 0)
    rr = lax.broadcasted_iota(jnp.int32, (tq, width), 1) & (tq - 1)

    bias = bias_sc[...]
    q3_slc = jnp.concatenate([q_top, jnp.concatenate([bias] * NSA_HPG, axis=1)], axis=0)

    def slc_scores(t):
        return _dot(kvs_ref[pl.ds(pl.multiple_of(t * tq, tq), tq), :], q3_slc)

    def slc_values(t):
        return vst_ref[:, pl.ds(pl.multiple_of(t * tq, tq), tq)]

    o_slc = _flash_tiles(0, i, slc_scores, slc_values, (sa_sc, sb_sc), acc_sc, width, cc <= rr, unroll=4)

    def win_scores(t):
        return _dot(kvw_ref[pl.ds(pl.multiple_of(t * tq, tq), tq), :], q3)

    def win_values(t):
        return vwt_ref[:, pl.ds(pl.multiple_of(t * tq, tq), tq)]

    back = NSA_WINDOW // tq
    o_win = _flash_tiles(jnp.maximum(i - back, 0), i, win_scores, win_values, (sa_sc, sb_sc), acc_sc, width,
                         cc <= rr, first_mask=jnp.logical_or(cc > rr, i < back))

    gt = gt_ref[...]
    outs = []
    for h in range(NSA_HPG):
        sl = slice(h * tq, (h + 1) * tq)
        outs.append(gt[3 * h:3 * h + 1] * o_cmp[:, sl] + gt[3 * h + 1:3 * h + 2] * o_slc[:, sl]
                    + gt[3 * h + 2:3 * h + 3] * o_win[:, sl])
    outs.append(jnp.zeros((NSA_DH, tq), F32))
    o_ref[...] = jnp.concatenate(outs, axis=0).T.astype(BF16)


def _nsa_attention(q6t, kvcmp, vct, kvs, vst, kvw, vwt, gate_t, ovt, batch, seq):
    tq = NSA_TILE
    nq = seq // tq
    ncp = seq // NSA_CMP_STRIDE
    n_slc = seq // NSA_SLC_LEN
    width = NSA_HPG * tq
    kern = functools.partial(_nsa_kernel, tq=tq, n_slc=n_slc, n_sel=min(NSA_TOPN, n_slc))
    kv_spec = pl.BlockSpec((seq, LANES), lambda b, g, i: (b, g))
    vt_spec = pl.BlockSpec((V_ROWS, seq), lambda b, g, i: (g, b))
    return pl.pallas_call(
        kern,
        grid=(batch, NSA_KV_HEADS, nq),
        in_specs=[pl.BlockSpec((NSA_HPG * NSA_DH, tq), lambda b, g, i: (g, b * nq + i)),
                  pl.BlockSpec((ncp, LANES), lambda b, g, i: (b, g)),
                  pl.BlockSpec((V_ROWS, ncp), lambda b, g, i: (b * NSA_KV_HEADS + g, 0)),
                  kv_spec, vt_spec, kv_spec, vt_spec,
                  pl.BlockSpec((LANES, tq), lambda b, g, i: (g, b * nq + i)),
                  _const_spec(ovt.shape)],
        out_specs=pl.BlockSpec((tq, 2 * LANES), lambda b, g, i: (b * nq + i, g)),
        out_shape=jax.ShapeDtypeStruct((batch * seq, NSA_KV_HEADS * 2 * LANES), BF16),
        scratch_shapes=[pltpu.VMEM((NSA_DH, tq), BF16), pltpu.VMEM((NSA_DH, width), F32),
                        pltpu.VMEM((V_ROWS, width), F32), pltpu.VMEM((tq, width), F32), pltpu.VMEM((tq, width), F32)],
        compiler_params=_params(("parallel", "parallel", "arbitrary")),
        name="nsa_attention",
    )(q6t, kvcmp, vct, kvs, vst, kvw, vwt, gate_t, ovt)


def _mla_kernel(qt_ref, k_ref, vt_ref, o_ref, acc_sc, sa_sc, sb_sc, *, tq):
    i = pl.program_id(2)
    causal = lax.broadcasted_iota(jnp.int32, (tq, tq), 0) <= lax.broadcasted_iota(jnp.int32, (tq, tq), 1)
    outs = []
    for hh in range(2):
        def scores(t, hh=hh):
            off = pl.multiple_of(t * tq, tq)
            return _dot(k_ref[pl.ds(off, tq), hh * LANES:(hh + 1) * LANES], qt_ref[hh * LANES:(hh + 1) * LANES, :])

        def values(t, hh=hh):
            return vt_ref[hh * V_ROWS:(hh + 1) * V_ROWS, pl.ds(pl.multiple_of(t * tq, tq), tq)]

        outs.append(_flash_tiles(0, i, scores, values, (sa_sc, sb_sc), acc_sc, tq, causal, unroll=4))
    o_ref[...] = jnp.concatenate(outs, axis=0).T.astype(BF16)


def _mla_attention(mqt, mk, mvt, batch, seq):
    tq = MLA_TILE
    nq = seq // tq
    pairs = MLA_HEADS // 2
    return pl.pallas_call(
        functools.partial(_mla_kernel, tq=tq),
        grid=(batch, pairs, nq),
        in_specs=[pl.BlockSpec((2 * LANES, tq), lambda b, p, i: (p, b * nq + i)),
                  pl.BlockSpec((seq, 2 * LANES), lambda b, p, i: (b, p)),
                  pl.BlockSpec((2 * V_ROWS, seq), lambda b, p, i: (p, b))],
        out_specs=pl.BlockSpec((tq, LANES), lambda b, p, i: (b * nq + i, p)),
        out_shape=jax.ShapeDtypeStruct((batch * seq, pairs * LANES), BF16),
        scratch_shapes=[pltpu.VMEM((V_ROWS, tq), F32), pltpu.VMEM((tq, tq), F32), pltpu.VMEM((tq, tq), F32)],
        compiler_params=_params(("parallel", "parallel", "arbitrary")),
        name="mla_attention",
    )(mqt, mk, mvt)


def _out_ffn_kernel(x_ref, on_ref, om_ref, u_ref, uh_ref, pw_ref, ps_ref, won_ref, wom_ref, wop_ref, g2_ref,
                    w1_ref, w2_ref, gf_ref, y_ref, *, tm, seq, final):
    i = pl.program_id(0)
    t0 = (i * tm) % seq
    u = u_ref[...]
    halo = jnp.where(t0 == 0, 0.0, uh_ref[...])
    ext = jnp.concatenate([halo, u], axis=0)
    s2 = ext + pltpu.roll(ext, 1, axis=0)
    s4 = s2 + pltpu.roll(s2, 2, axis=0)
    s8 = s4 + pltpu.roll(s4, 4, axis=0)
    s16 = s8 + pltpu.roll(s8, 8, axis=0)
    sl = slice(POOL_HALO, POOL_HALO + tm)
    lane = lax.broadcasted_iota(jnp.int32, (tm, POOL_WIDTH), 1)
    t = t0 + lax.broadcasted_iota(jnp.int32, (tm, POOL_WIDTH), 0)
    sums = jnp.where(lane < 64, s2[sl], jnp.where(lane < 128, s4[sl], jnp.where(lane < 192, s8[sl], s16[sl])))
    win = jnp.where(lane < 64, 2, jnp.where(lane < 128, 4, jnp.where(lane < 192, 8, 16)))
    cnt = jnp.minimum(t + 1, win).astype(F32)
    pooled = (sums / cnt - u).astype(BF16)
    y_pool = (_dot(pooled, pw_ref[...]) * ps_ref[...]).astype(BF16)
    mix = _dot(on_ref[...], won_ref[...]) + _dot(om_ref[...], wom_ref[...]) + _dot(y_pool, wop_ref[...])
    x = x_ref[...] + mix
    h = _rms(x, g2_ref[...]).astype(BF16)
    ff = None
    for c in range(D_FF // FF_CHUNK):
        cs = slice(c * FF_CHUNK, (c + 1) * FF_CHUNK)
        a = jnp.maximum(_dot(h, w1_ref[:, cs]), 0.0)
        part = _dot((a * a).astype(BF16), w2_ref[cs, :])
        ff = part if ff is None else ff + part
    acc = x + ff
    if final:
        acc = _rms(acc, gf_ref[...])
    y_ref[...] = acc


def _out_ffn(x, o_nsa, o_mla, u, pw, ps, won, wom, wop, g2, w1, w2, gf, seq, final):
    n = x.shape[0]
    tm = ROW_TILE
    hb = tm // POOL_HALO
    row = lambda width: pl.BlockSpec((tm, width), lambda i: (i, 0))
    return pl.pallas_call(
        functools.partial(_out_ffn_kernel, tm=tm, seq=seq, final=final),
        grid=(n // tm,),
        in_specs=[row(D_MODEL), row(o_nsa.shape[1]), row(o_mla.shape[1]), row(POOL_WIDTH),
                  pl.BlockSpec((POOL_HALO, POOL_WIDTH), lambda i: (jnp.maximum(i * hb - 1, 0), 0)),
                  _const_spec(pw.shape), _const_spec(ps.shape), _const_spec(won.shape), _const_spec(wom.shape),
                  _const_spec(wop.shape), _const_spec(g2.shape), _const_spec(w1.shape), _const_spec(w2.shape),
                  _const_spec(gf.shape)],
        out_specs=row(D_MODEL),
        out_shape=jax.ShapeDtypeStruct((n, D_MODEL), F32),
        compiler_params=_params(("parallel",)),
        name="out_ffn",
    )(x, o_nsa, o_mla, u, u, pw, ps, won, wom, wop, g2, w1, w2, gf)


def _overlap_t(seq):
    ncp = seq // NSA_CMP_STRIDE
    n_slc = seq // NSA_SLC_LEN
    start = np.arange(ncp - 1)[None, :] * NSA_CMP_STRIDE
    end = start + NSA_CMP_LEN
    s0 = np.arange(n_slc)[:, None] * NSA_SLC_LEN
    s1 = s0 + NSA_SLC_LEN
    ov = np.clip(np.minimum(end, s1) - np.maximum(start, s0), 0, None) / NSA_CMP_LEN
    out = np.zeros((LANES, ncp), np.float32)
    out[:n_slc, :ncp - 1] = ov
    return jnp.asarray(out, BF16)


def kernel(x, positions, ln1_g, w_in, nsa_cmp_pos, nsa_cmp_w1, nsa_cmp_w2, mla_q_norm, mla_w_qup, mla_kv_norm,
           mla_w_kvup, pool_w, pool_scale, w_out, ln2_g, w_ff1, w_ff2, final_g):
    batch, seq, _ = x.shape
    depth = w_in.shape[0]
    n = batch * seq
    assert n % min(ROPE_TILE, n) == 0 and n % ROW_TILE == 0
    assert seq % ROW_TILE == 0 and seq % NSA_TILE == 0 and seq % MLA_TILE == 0
    assert seq // NSA_SLC_LEN <= NSA_DH and NSA_WINDOW % NSA_TILE == 0 and NSA_TILE % NSA_SLC_LEN == 0

    cos_t, sin_t = _rope_tables(positions)
    ovt = _overlap_t(seq)
    in_layout, in_layout_t = _in_layout(), _in_layout_t()
    lq, lqr, lk, lv = _mla_layouts()
    won_layout = _layout([r for g in range(NSA_KV_HEADS)
                          for r in ((g * NSA_HPG * NSA_DH, NSA_HPG * NSA_DH, 1.0), (None, NSA_DH, 0.0))])
    row2 = lambda v: v.reshape(1, -1)

    xf = x.reshape(n, D_MODEL)
    for l in range(depth):
        w_all = _take_cols(w_in[l], in_layout)
        wt_all = _take_cols(w_in[l], in_layout_t).T
        wqt, wqrt = _take_cols(mla_w_qup[l], lq).T, _take_cols(mla_w_qup[l], lqr).T
        wk, wvt = _take_cols(mla_w_kvup[l], lk), _take_cols(mla_w_kvup[l], lv).T
        q6t, vst, vwt, gate_t, kvc, kvs, kvw, mqt, mk, mvt, u = _proj_in(
            xf, row2(ln1_g[l]), w_all, wt_all, cos_t, sin_t, row2(mla_q_norm[l]), row2(mla_kv_norm[l]),
            wqt, wqrt, wk, wvt, seq)

        w1c = _blockdiag2(nsa_cmp_w1[l, 0], nsa_cmp_w1[l, 1]).astype(BF16)
        w2c = _blockdiag2(nsa_cmp_w2[l, 0], nsa_cmp_w2[l, 1]).astype(BF16)
        pos = jnp.concatenate([nsa_cmp_pos[l, 0], nsa_cmp_pos[l, 1]], axis=-1)[:, None, :]
        half = NSA_CMP_STRIDE
        kvcmp, vct = _compress(kvc, pos[:half], pos[half:], w1c[:half], w1c[half:], w2c, batch, seq)

        o_nsa = _nsa_attention(q6t, kvcmp, vct, kvs, vst, kvw, vwt, gate_t, ovt, batch, seq)
        o_mla = _mla_attention(mqt, mk, mvt, batch, seq)

        pw = _blockdiag2(_blockdiag2(pool_w[l, 0], pool_w[l, 1]), _blockdiag2(pool_w[l, 2], pool_w[l, 3])).astype(BF16)
        won = _take_rows(w_out[l], won_layout)
        wom = w_out[l, 384:768].astype(BF16)
        wop = w_out[l, 768:1024].astype(BF16)
        xf = _out_ffn(xf, o_nsa, o_mla, u, pw, row2(pool_scale[l]), won, wom, wop, row2(ln2_g[l]),
                      w_ff1[l].astype(BF16), w_ff2[l].astype(BF16), row2(final_g), seq, final=(l == depth - 1))
    return xf.reshape(batch, seq, D_MODEL)
```

```python
import functools

import numpy as np
import jax
import jax.numpy as jnp
from jax import lax
from jax.experimental import pallas as pl
from jax.experimental.pallas import tpu as pltpu

F32 = jnp.float32
BF16 = jnp.bfloat16

D_MODEL = 1024
NSA_HEADS = 6
NSA_KV_HEADS = 2
NSA_HPG = 3
NSA_DH = 64
NSA_CMP_LEN = 32
NSA_CMP_STRIDE = 16
NSA_SLC_LEN = 64
NSA_TOPN = 16
NSA_WINDOW = 512
NSA_FORCE = 1.0e4
MLA_HEADS = 6
MLA_NOPE = 64
MLA_ROPE = 32
MLA_VDIM = 64
V_ROWS = 80
ROPE_THETA = 10000.0
POOL_WIDTH = 256
POOL_HALO = 16
D_FF = 4096
EPS = 1e-6

LANES = 128
SUBLANES = 8
VMEM_LIMIT = 56 * 1024 * 1024

MASKED = -1.0e30
M_FLOOR = -1.0e20
LOG2E = 1.4426950408889634

ROW_TILE = 512
ROPE_TILE = 4096
NSA_TILE = 256
MLA_TILE = 512
FF_CHUNK = 1024
SELECT_VARIANTS = 4


def _dot(a, b):
    return jnp.dot(a, b, preferred_element_type=F32)


def _dot_nt(a, b):
    return lax.dot_general(a, b, (((1,), (1,)), ((), ())), preferred_element_type=F32)


def _rms(x, g):
    return x * lax.rsqrt(jnp.mean(x * x, axis=-1, keepdims=True) + EPS) * g


def _layout(spec):
    idx, scl = [], []
    for start, n, s in spec:
        if start is None:
            idx.extend([0] * n)
            scl.extend([0.0] * n)
        else:
            idx.extend(range(start, start + n))
            scl.extend([s] * n)
    return np.asarray(idx, np.int32), np.asarray(scl, np.float32)


def _nkv_col(br, kvi, g):
    return 384 + br * 256 + kvi * 128 + g * 64


def _in_layout():
    spec = []
    for br in range(3):
        for g in range(2):
            spec += [(_nkv_col(br, 0, g), 64, 1.0)]
            spec += [(None, 64, 0.0)] if br == 1 else [(_nkv_col(br, 1, g), 64, 1.0)]
    spec += [(1170, 256, 1.0), (1426, 128, 1.0)]
    spec += [(1586, 256, 1.0)]
    return _layout(spec)


def _in_layout_t():
    spec = []
    spec += [(0, NSA_HEADS * NSA_DH, 0.125)]
    for br in (1, 2):
        for g in range(2):
            spec += [(_nkv_col(br, 1, g), 64, 1.0), (None, V_ROWS - 64, 0.0)]
    for g in range(2):
        spec += [(1152 + g * 9, 9, 1.0), (None, LANES - 9, 0.0)]
    spec += [(None, 64, 0.0), (1554, 32, 1.0), (None, 32, 0.0)]
    spec += [(None, 64, 0.0), (1570, 16, -1.0), (1554, 16, 1.0), (None, 32, 0.0)]
    return _layout(spec)


def _mla_layouts():
    q, qr, k, v = [], [], [], []
    for h in range(MLA_HEADS):
        b = h * 96
        q += [(b, 96, 1.0), (None, 32, 0.0)]
        qr += [(None, 64, 0.0), (b + 80, 16, -1.0), (b + 64, 16, 1.0), (None, 32, 0.0)]
        k += [(h * 128, 64, 1.0), (None, 64, 0.0)]
        v += [(h * 128 + 64, 64, 1.0), (None, V_ROWS - 64, 0.0)]
    return _layout(q), _layout(qr), _layout(k), _layout(v)


def _take_cols(w, layout):
    idx, scl = layout
    return (jnp.take(w, idx, axis=1) * scl[None, :]).astype(BF16)


def _take_rows(w, layout):
    idx, scl = layout
    return (jnp.take(w, idx, axis=0) * scl[:, None]).astype(BF16)


def _blockdiag2(a, b):
    z = jnp.zeros_like(a)
    top = jnp.concatenate([a, z], axis=-1)
    bot = jnp.concatenate([z, b], axis=-1)
    return jnp.concatenate([top, bot], axis=-2)


def _const_spec(shape):
    nd = len(shape)
    return pl.BlockSpec(shape, lambda *_: (0,) * nd, pipeline_mode=pl.Buffered(1))


def _params(sem):
    return pltpu.CompilerParams(dimension_semantics=sem, vmem_limit_bytes=VMEM_LIMIT)


def _rope_kernel(pos_ref, inv_ref, cos_ref, sin_ref):
    ang = inv_ref[...] * pos_ref[...].astype(F32)
    cos_ref[...] = jnp.cos(ang)
    sin_ref[...] = jnp.sin(ang)


def _rope_tables(positions):
    n = positions.size
    half = MLA_ROPE // 2
    inv = jnp.power(jnp.float32(ROPE_THETA), -(jnp.arange(half, dtype=F32) / half))
    tm = min(ROPE_TILE, n)
    out = jax.ShapeDtypeStruct((half, n), F32)
    return pl.pallas_call(
        _rope_kernel,
        grid=(n // tm,),
        in_specs=[pl.BlockSpec((1, tm), lambda i: (0, i)), _const_spec((half, 1))],
        out_specs=[pl.BlockSpec((half, tm), lambda i: (0, i))] * 2,
        out_shape=[out, out],
        compiler_params=_params(("parallel",)),
        name="rope_tables",
    )(positions.reshape(1, n), inv.reshape(half, 1))


def _ones_rows(rows, cols):
    r = lax.broadcasted_iota(jnp.int32, (rows, cols), 0) % V_ROWS
    return jnp.where(r >= 64, 1.0, 0.0)


def _proj_in_kernel(x_ref, g_ref, w_ref, wt_ref, cost_ref, sint_ref, qg_ref, kvg_ref,
                    wqt_ref, wqrt_ref, wk_ref, wvt_ref,
                    q6t_ref, vst_ref, vwt_ref, gatet_ref, kvc_ref, kvs_ref, kvw_ref, mqt_ref, mk_ref, mvt_ref, u_ref,
                    *, seq):
    h = _rms(x_ref[...], g_ref[...]).astype(BF16)
    zt = _dot_nt(wt_ref[...], h)
    tm = h.shape[0]
    nv = NSA_KV_HEADS * V_ROWS
    nq = NSA_HEADS * NSA_DH
    q6t_ref[...] = (zt[0:nq] * LOG2E).astype(BF16)
    vst_ref[...] = (zt[nq:nq + nv] + _ones_rows(nv, tm)).astype(BF16)
    vwt_ref[...] = (zt[nq + nv:nq + 2 * nv] + _ones_rows(nv, tm)).astype(BF16)
    gatet_ref[...] = jax.nn.sigmoid(zt[nq + 2 * nv:nq + 2 * nv + 256])
    kr_t = zt[nq + 2 * nv + 256:nq + 2 * nv + 384]
    krrot_t = zt[nq + 2 * nv + 384:nq + 2 * nv + 512]
    z = _dot(h, w_ref[:, 0:768])
    kvc_ref[...] = z[:, 0:256]
    tok = (pl.program_id(0) * tm) % seq + lax.broadcasted_iota(jnp.int32, (tm, 2 * LANES), 0)
    lane = lax.broadcasted_iota(jnp.int32, (tm, 2 * LANES), 1) & (LANES - 1)
    onehot = jnp.where(lane - NSA_DH == tok // NSA_SLC_LEN, 1.0, 0.0)
    kvs_ref[...] = (z[:, 256:512] + onehot).astype(BF16)
    kvw_ref[...] = z[:, 512:768].astype(BF16)
    z = _dot(h, w_ref[:, 768:1408])
    u_ref[...] = z[:, 384:640]
    cqn = _rms(z[:, 0:256], qg_ref[...]).astype(BF16)
    qa = _dot_nt(wqt_ref[...], cqn)
    qb = _dot_nt(wqrt_ref[...], cqn)
    c16, s16 = cost_ref[...], sint_ref[...]
    cos_t = jnp.concatenate([jnp.ones((64, tm), F32), c16, c16, jnp.ones((32, tm), F32)], axis=0)
    sin_t = jnp.concatenate([jnp.zeros((64, tm), F32), s16, s16, jnp.zeros((32, tm), F32)], axis=0)
    scale = (MLA_NOPE + MLA_ROPE) ** -0.5 * LOG2E
    for hh in range(MLA_HEADS):
        sl = slice(hh * LANES, (hh + 1) * LANES)
        mqt_ref[sl, :] = ((qa[sl] * cos_t + qb[sl] * sin_t) * scale).astype(BF16)
    ckvn = _rms(z[:, 256:384], kvg_ref[...]).astype(BF16)
    k_rope = (kr_t * cos_t + krrot_t * sin_t).T
    ka = _dot(ckvn, wk_ref[...])
    for hh in range(MLA_HEADS):
        sl = slice(hh * LANES, (hh + 1) * LANES)
        mk_ref[:, sl] = (ka[:, sl] + k_rope).astype(BF16)
    mvt_ref[...] = (_dot_nt(wvt_ref[...], ckvn) + _ones_rows(MLA_HEADS * V_ROWS, tm)).astype(BF16)


def _proj_in(x, g, w_all, wt_all, cos_t, sin_t, qg, kvg, wqt, wqrt, wk, wvt, seq):
    n = x.shape[0]
    tm = ROW_TILE
    row = lambda width: pl.BlockSpec((tm, width), lambda i: (i, 0))
    col = lambda height: pl.BlockSpec((height, tm), lambda i: (0, i))
    nv = NSA_KV_HEADS * V_ROWS
    outs = [("t", NSA_HEADS * NSA_DH, BF16), ("t", nv, BF16), ("t", nv, BF16), ("t", 256, F32), ("r", 256, F32),
            ("r", 256, BF16), ("r", 256, BF16), ("t", 768, BF16), ("r", 768, BF16), ("t", MLA_HEADS * V_ROWS, BF16),
            ("r", 256, F32)]
    return pl.pallas_call(
        functools.partial(_proj_in_kernel, seq=seq),
        grid=(n // tm,),
        in_specs=[row(D_MODEL), _const_spec(g.shape), _const_spec(w_all.shape), _const_spec(wt_all.shape),
                  col(MLA_ROPE // 2), col(MLA_ROPE // 2), _const_spec(qg.shape), _const_spec(kvg.shape),
                  _const_spec(wqt.shape), _const_spec(wqrt.shape), _const_spec(wk.shape), _const_spec(wvt.shape)],
        out_specs=[col(w) if kind == "t" else row(w) for kind, w, _ in outs],
        out_shape=[jax.ShapeDtypeStruct((w, n) if kind == "t" else (n, w), dt) for kind, w, dt in outs],
        compiler_params=_params(("parallel",)),
        name="proj_in",
    )(x, g, w_all, wt_all, cos_t, sin_t, qg, kvg, wqt, wqrt, wk, wvt)


def _compress_kernel(kvc_ref, posa_ref, posb_ref, wa_ref, wb_ref, w2_ref, o_ref, vt_ref, *, ncp):
    a = jnp.zeros((ncp, LANES), F32)
    b = jnp.zeros((ncp, LANES), F32)
    for l in range(NSA_CMP_STRIDE):
        xl = kvc_ref[pl.ds(l, ncp, stride=NSA_CMP_STRIDE), :]
        a = a + _dot((xl + posa_ref[l]).astype(BF16), wa_ref[l])
        b = b + _dot((xl + posb_ref[l]).astype(BF16), wb_ref[l])
    pre = a + pltpu.roll(b, ncp - 1, axis=0)
    hid = jax.nn.gelu(pre).astype(BF16)
    out = _dot(hid, w2_ref[...])
    o_ref[...] = out.astype(BF16)
    ones = jnp.ones((V_ROWS - NSA_DH, ncp), F32)
    vt_ref[...] = jnp.concatenate([out.T[NSA_DH:2 * NSA_DH], ones], axis=0).astype(BF16)


def _compress(kvc, posa, posb, wa, wb, w2, batch, seq):
    ncp = seq // NSA_CMP_STRIDE
    return pl.pallas_call(
        functools.partial(_compress_kernel, ncp=ncp),
        grid=(batch, NSA_KV_HEADS),
        in_specs=[pl.BlockSpec((seq, LANES), lambda b, g: (b, g)), _const_spec(posa.shape), _const_spec(posb.shape),
                  _const_spec(wa.shape), _const_spec(wb.shape), _const_spec(w2.shape)],
        out_specs=[pl.BlockSpec((ncp, LANES), lambda b, g: (b, g)),
                   pl.BlockSpec((V_ROWS, ncp), lambda b, g: (b * NSA_KV_HEADS + g, 0))],
        out_shape=[jax.ShapeDtypeStruct((batch * ncp, 2 * LANES), BF16),
                   jax.ShapeDtypeStruct((batch * NSA_KV_HEADS * V_ROWS, ncp), BF16)],
        compiler_params=_params(("parallel", "parallel")),
        name="nsa_compress",
    )(kvc, posa, posb, wa, wb, w2)


def _flash_update(buf, v_t, m, acc_ref, mask=None):
    load = (lambda: buf[...]) if mask is None else (lambda: jnp.where(mask, buf[...], MASKED))
    m_new = jnp.maximum(m, jnp.max(load(), axis=0, keepdims=True))
    alpha = jnp.exp2(m - m_new)
    p = jnp.exp2(load() - m_new)
    acc_ref[...] = alpha * acc_ref[...] + _dot(v_t, p.astype(BF16))
    return m_new


class _FlashChain:
    def __init__(self, first, last, scores, values, bufs, acc_ref, width, unroll=2):
        self.first, self.last, self.scores, self.values = first, last, scores, values
        self.bufs, self.acc_ref, self.width, self.unroll = bufs, acc_ref, width, unroll
        self.m = None

    def start(self, first_mask=None):
        s_first = self.scores(self.first)
        self.bufs[0][...] = s_first if first_mask is None else jnp.where(first_mask, s_first, MASKED)
        self.acc_ref[...] = jnp.zeros(self.acc_ref.shape, F32)
        self.m = jnp.full((1, self.width), M_FLOOR, F32)

    def loops(self):
        bufs, acc_ref, scores, values = self.bufs, self.acc_ref, self.scores, self.values

        def steps(count, t0):
            def body(jj, m):
                for k in range(count):
                    t = t0 + count * jj + k
                    cur = bufs[k % 2]
                    if count > 1:
                        bufs[(k + 1) % 2][...] = scores(t + 1)
                    m = _flash_update(cur, values(t), m, acc_ref)
                    if count == 1:
                        cur[...] = scores(t + 1)
                return m
            return body

        t0, rem, level = self.first, self.last - self.first, self.unroll
        while level >= 1:
            trips = rem // level
            self.m = lax.fori_loop(0, trips, steps(level, t0), self.m)
            t0, rem, level = t0 + trips * level, rem - trips * level, level // 2

    def finish(self, last_mask):
        _flash_update(self.bufs[0], self.values(self.last), self.m, self.acc_ref, mask=last_mask)
        return self.acc_ref[0:64] * (1.0 / self.acc_ref[64:65])


def _nsa_select(q3, q0, kvc_ref, vct_ref, ovt_ref, ocmp_ref, bias_ref, *, rows, ngroups, tq, n_slc, n_sel):
    width = NSA_HPG * tq
    ncp = kvc_ref.shape[0]
    s = _dot(kvc_ref[0:rows, :], q3)
    nblk = lax.broadcasted_iota(jnp.int32, (rows, width), 0)
    qpos = q0 + (lax.broadcasted_iota(jnp.int32, (1, width), 1) & (tq - 1))
    last_blk = (qpos - (NSA_CMP_LEN - 1)) >> 4
    s = jnp.where(nblk <= last_blk, s, -jnp.inf)
    m = jnp.max(s, axis=0, keepdims=True)
    m = jnp.where(m == -jnp.inf, 0.0, m)
    e = jnp.exp2(s - m)
    d = jnp.sum(e, axis=0, keepdims=True)
    p = e * (1.0 / jnp.where(d > 0.0, d, 1.0))

    def pad(a):
        return a if rows == ncp else jnp.concatenate([a, jnp.zeros((ncp - rows, a.shape[1]), a.dtype)], axis=0)

    ocmp_ref[...] = _dot(vct_ref[0:NSA_DH, :], pad(p.astype(BF16)))

    every_valid_block_selected = q0 + tq <= n_sel * NSA_SLC_LEN

    @pl.when(every_valid_block_selected)
    def _():
        bias_ref[...] = jnp.zeros(bias_ref.shape, BF16)

    @pl.when(jnp.logical_not(every_valid_block_selected))
    def _():
        psum = p[:, 0:tq] + p[:, tq:2 * tq] + p[:, 2 * tq:3 * tq]
        hi = psum.astype(BF16)
        lo = (psum - hi.astype(F32)).astype(BF16)
        ovt = ovt_ref[...]
        imp = _dot(ovt, pad(hi)) + _dot(ovt, pad(lo))
        nblocks = ngroups * SUBLANES
        blk = lax.broadcasted_iota(jnp.int32, (LANES, tq), 0)
        qp = q0 + lax.broadcasted_iota(jnp.int32, (LANES, tq), 1)
        cur = qp >> 6
        forced = (blk == 0) | (blk == cur) | (blk == cur - 1)
        valid = (blk << 6) <= qp
        score = jnp.where(forced, NSA_FORCE, jnp.where(valid, imp, -1.0))[0:nblocks]
        sub = SUBLANES
        sidx = lax.broadcasted_iota(jnp.int32, (sub, tq), 0)
        groups = [score[k * sub:(k + 1) * sub] for k in range(ngroups)]
        cnts = [jnp.zeros((sub, tq), F32) for _ in groups]
        for sp in range(nblocks):
            col = score[sp:sp + 1, :]
            for k, grp in enumerate(groups):
                if k * sub > sp:
                    cnts[k] = cnts[k] + jnp.where(col >= grp, 1.0, 0.0)
                elif (k + 1) * sub - 1 <= sp:
                    cnts[k] = cnts[k] + jnp.where(col > grp, 1.0, 0.0)
                else:
                    after = jnp.where(col >= grp, 1.0, 0.0)
                    cnts[k] = cnts[k] + jnp.where(sidx + k * sub > sp, after, jnp.where(col > grp, 1.0, 0.0))
        bias = jnp.where(jnp.concatenate(cnts, axis=0) < n_sel, 0.0, MASKED)
        if nblocks < NSA_DH:
            bias = jnp.concatenate([bias, jnp.zeros((NSA_DH - nblocks, tq), F32)], axis=0)
        bias_ref[...] = bias.astype(BF16)


def _nsa_kernel(qt_ref, kvc_ref, vct_ref, kvs_ref, vst_ref, kvw_ref, vwt_ref, gt_ref, ovt_ref, o_ref,
                bias_sc, ocmp_sc, acc_sc, sa_sc, sb_sc, wacc_sc, wa_sc, wb_sc, *, tq, n_slc, n_sel):
    nq = qt_ref.shape[1] // tq

    def q_tile(i, carry):
        _nsa_q_tile(i, qt_ref, kvc_ref, vct_ref, kvs_ref, vst_ref, kvw_ref, vwt_ref, gt_ref, ovt_ref, o_ref,
                    bias_sc, ocmp_sc, acc_sc, sa_sc, sb_sc, wacc_sc, wa_sc, wb_sc, tq=tq, n_slc=n_slc, n_sel=n_sel)
        return carry

    lax.fori_loop(0, nq, q_tile, 0)


def _nsa_q_tile(i, qt_ref, kvc_ref, vct_ref, kvs_ref, vst_ref, kvw_ref, vwt_ref, gt_ref, ovt_ref, o_ref,
                bias_sc, ocmp_sc, acc_sc, sa_sc, sb_sc, wacc_sc, wa_sc, wb_sc, *, tq, n_slc, n_sel):
    q0 = pl.multiple_of(i * tq, tq)
    width = NSA_HPG * tq
    q_top = jnp.concatenate([qt_ref[h * NSA_DH:(h + 1) * NSA_DH, pl.ds(q0, tq)] for h in range(NSA_HPG)], axis=1)
    q3 = jnp.concatenate([q_top, jnp.zeros((NSA_DH, width), BF16)], axis=0)

    ncp = kvc_ref.shape[0]
    nq = ncp * NSA_CMP_STRIDE // tq
    divisible = all(v % SELECT_VARIANTS == 0 for v in (nq, ncp // 16, n_slc // SUBLANES))
    nvar = SELECT_VARIANTS if divisible else 1
    for c in range(nvar):
        @pl.when((i * nvar) // nq == c)
        def _(c=c):
            _nsa_select(q3, q0, kvc_ref, vct_ref, ovt_ref, ocmp_sc, bias_sc, rows=ncp * (c + 1) // nvar,
                        ngroups=(n_slc // SUBLANES) * (c + 1) // nvar, tq=tq, n_slc=n_slc, n_sel=n_sel)
    o_cmp = ocmp_sc[...]

    cc = lax.broadcasted_iota(jnp.int32, (tq, width), 0)
    rr = lax.broadcasted_iota(jnp.int32, (tq, width), 1) & (tq - 1)

    bias = bias_sc[...]
    q3_slc = jnp.concatenate([q_top, jnp.concatenate([bias] * NSA_HPG, axis=1)], axis=0)

    def slc_scores(t):
        return _dot(kvs_ref[pl.ds(pl.multiple_of(t * tq, tq), tq), :], q3_slc)

    def slc_values(t):
        return vst_ref[:, pl.ds(pl.multiple_of(t * tq, tq), tq)]

    def win_scores(t):
        return _dot(kvw_ref[pl.ds(pl.multiple_of(t * tq, tq), tq), :], q3)

    def win_values(t):
        return vwt_ref[:, pl.ds(pl.multiple_of(t * tq, tq), tq)]

    back = NSA_WINDOW // tq
    slc = _FlashChain(0, i, slc_scores, slc_values, (sa_sc, sb_sc), acc_sc, width, unroll=4)
    win = _FlashChain(jnp.maximum(i - back, 0), i, win_scores, win_values, (wa_sc, wb_sc), wacc_sc, width)
    slc.start()
    win.start(first_mask=jnp.logical_or(cc > rr, i < back))
    slc.loops()
    win.loops()
    o_slc = slc.finish(cc <= rr)
    o_win = win.finish(cc <= rr)

    gt = gt_ref[:, pl.ds(q0, tq)]
    outs = []
    for h in range(NSA_HPG):
        sl = slice(h * tq, (h + 1) * tq)
        outs.append(gt[3 * h:3 * h + 1] * o_cmp[:, sl] + gt[3 * h + 1:3 * h + 2] * o_slc[:, sl]
                    + gt[3 * h + 2:3 * h + 3] * o_win[:, sl])
    outs.append(jnp.zeros((NSA_DH, tq), F32))
    o_ref[pl.ds(q0, tq), :] = jnp.concatenate(outs, axis=0).T.astype(BF16)


def _nsa_attention(q6t, kvcmp, vct, kvs, vst, kvw, vwt, gate_t, ovt, batch, seq):
    tq = NSA_TILE
    ncp = seq // NSA_CMP_STRIDE
    n_slc = seq // NSA_SLC_LEN
    width = NSA_HPG * tq
    kern = functools.partial(_nsa_kernel, tq=tq, n_slc=n_slc, n_sel=min(NSA_TOPN, n_slc))
    kv_spec = pl.BlockSpec((seq, LANES), lambda b, g: (b, g))
    vt_spec = pl.BlockSpec((V_ROWS, seq), lambda b, g: (g, b))
    return pl.pallas_call(
        kern,
        grid=(batch, NSA_KV_HEADS),
        in_specs=[pl.BlockSpec((NSA_HPG * NSA_DH, seq), lambda b, g: (g, b)),
                  pl.BlockSpec((ncp, LANES), lambda b, g: (b, g)),
                  pl.BlockSpec((V_ROWS, ncp), lambda b, g: (b * NSA_KV_HEADS + g, 0)),
                  kv_spec, vt_spec, kv_spec, vt_spec,
                  pl.BlockSpec((LANES, seq), lambda b, g: (g, b)),
                  _const_spec(ovt.shape)],
        out_specs=pl.BlockSpec((seq, 2 * LANES), lambda b, g: (b, g)),
        out_shape=jax.ShapeDtypeStruct((batch * seq, NSA_KV_HEADS * 2 * LANES), BF16),
        scratch_shapes=[pltpu.VMEM((NSA_DH, tq), BF16), pltpu.VMEM((NSA_DH, width), F32),
                        pltpu.VMEM((V_ROWS, width), F32), pltpu.VMEM((tq, width), F32), pltpu.VMEM((tq, width), F32),
                        pltpu.VMEM((V_ROWS, width), F32), pltpu.VMEM((tq, width), F32), pltpu.VMEM((tq, width), F32)],
        compiler_params=_params(("parallel", "parallel")),
        name="nsa_attention",
    )(q6t, kvcmp, vct, kvs, vst, kvw, vwt, gate_t, ovt)


def _mla_kernel(qt_ref, k_ref, vt_ref, o_ref, acc0_sc, sa0_sc, sb0_sc, acc1_sc, sa1_sc, sb1_sc, *, tq):
    nq = qt_ref.shape[1] // tq
    causal = lax.broadcasted_iota(jnp.int32, (tq, tq), 0) <= lax.broadcasted_iota(jnp.int32, (tq, tq), 1)

    def q_tile(i, carry):
        q0 = pl.multiple_of(i * tq, tq)
        chains = []
        for hh, acc_sc, bufs in ((0, acc0_sc, (sa0_sc, sb0_sc)), (1, acc1_sc, (sa1_sc, sb1_sc))):
            def scores(t, hh=hh):
                off = pl.multiple_of(t * tq, tq)
                return _dot(k_ref[pl.ds(off, tq), hh * LANES:(hh + 1) * LANES],
                            qt_ref[hh * LANES:(hh + 1) * LANES, pl.ds(q0, tq)])

            def values(t, hh=hh):
                return vt_ref[hh * V_ROWS:(hh + 1) * V_ROWS, pl.ds(pl.multiple_of(t * tq, tq), tq)]

            chains.append(_FlashChain(0, i, scores, values, bufs, acc_sc, tq, unroll=4))
        for chain in chains:
            chain.start()
        for chain in chains:
            chain.loops()
        outs = [chain.finish(causal) for chain in chains]
        o_ref[pl.ds(q0, tq), :] = jnp.concatenate(outs, axis=0).T.astype(BF16)
        return carry

    lax.fori_loop(0, nq, q_tile, 0)


def _mla_attention(mqt, mk, mvt, batch, seq):
    tq = MLA_TILE
    pairs = MLA_HEADS // 2
    return pl.pallas_call(
        functools.partial(_mla_kernel, tq=tq),
        grid=(batch, pairs),
        in_specs=[pl.BlockSpec((2 * LANES, seq), lambda b, p: (p, b)),
                  pl.BlockSpec((seq, 2 * LANES), lambda b, p: (b, p)),
                  pl.BlockSpec((2 * V_ROWS, seq), lambda b, p: (p, b))],
        out_specs=pl.BlockSpec((seq, LANES), lambda b, p: (b, p)),
        out_shape=jax.ShapeDtypeStruct((batch * seq, pairs * LANES), BF16),
        scratch_shapes=[pltpu.VMEM((V_ROWS, tq), F32), pltpu.VMEM((tq, tq), F32), pltpu.VMEM((tq, tq), F32)] * 2,
        compiler_params=_params(("parallel", "parallel")),
        name="mla_attention",
    )(mqt, mk, mvt)


def _out_ffn_kernel(x_ref, on_ref, om_ref, u_ref, uh_ref, pw_ref, ps_ref, won_ref, wom_ref, wop_ref, g2_ref,
                    w1_ref, w2_ref, gf_ref, y_ref, *, tm, seq, final):
    i = pl.program_id(0)
    t0 = (i * tm) % seq
    u = u_ref[...]
    halo = jnp.where(t0 == 0, 0.0, uh_ref[...])
    ext = jnp.concatenate([halo, u], axis=0)
    s2 = ext + pltpu.roll(ext, 1, axis=0)
    s4 = s2 + pltpu.roll(s2, 2, axis=0)
    s8 = s4 + pltpu.roll(s4, 4, axis=0)
    s16 = s8 + pltpu.roll(s8, 8, axis=0)
    sl = slice(POOL_HALO, POOL_HALO + tm)
    lane = lax.broadcasted_iota(jnp.int32, (tm, POOL_WIDTH), 1)
    t = t0 + lax.broadcasted_iota(jnp.int32, (tm, POOL_WIDTH), 0)
    sums = jnp.where(lane < 64, s2[sl], jnp.where(lane < 128, s4[sl], jnp.where(lane < 192, s8[sl], s16[sl])))
    win = jnp.where(lane < 64, 2, jnp.where(lane < 128, 4, jnp.where(lane < 192, 8, 16)))
    cnt = jnp.minimum(t + 1, win).astype(F32)
    pooled = (sums / cnt - u).astype(BF16)
    y_pool = (_dot(pooled, pw_ref[...]) * ps_ref[...]).astype(BF16)
    mix = _dot(on_ref[...], won_ref[...]) + _dot(om_ref[...], wom_ref[...]) + _dot(y_pool, wop_ref[...])
    x = x_ref[...] + mix
    h = _rms(x, g2_ref[...]).astype(BF16)
    ff = None
    for c in range(D_FF // FF_CHUNK):
        cs = slice(c * FF_CHUNK, (c + 1) * FF_CHUNK)
        a = jnp.maximum(_dot(h, w1_ref[:, cs]), 0.0)
        part = _dot((a * a).astype(BF16), w2_ref[cs, :])
        ff = part if ff is None else ff + part
    acc = x + ff
    if final:
        acc = _rms(acc, gf_ref[...])
    y_ref[...] = acc


def _out_ffn(x, o_nsa, o_mla, u, pw, ps, won, wom, wop, g2, w1, w2, gf, seq, final):
    n = x.shape[0]
    tm = ROW_TILE
    hb = tm // POOL_HALO
    row = lambda width: pl.BlockSpec((tm, width), lambda i: (i, 0))
    return pl.pallas_call(
        functools.partial(_out_ffn_kernel, tm=tm, seq=seq, final=final),
        grid=(n // tm,),
        in_specs=[row(D_MODEL), row(o_nsa.shape[1]), row(o_mla.shape[1]), row(POOL_WIDTH),
                  pl.BlockSpec((POOL_HALO, POOL_WIDTH), lambda i: (jnp.maximum(i * hb - 1, 0), 0)),
                  _const_spec(pw.shape), _const_spec(ps.shape), _const_spec(won.shape), _const_spec(wom.shape),
                  _const_spec(wop.shape), _const_spec(g2.shape), _const_spec(w1.shape), _const_spec(w2.shape),
                  _const_spec(gf.shape)],
        out_specs=row(D_MODEL),
        out_shape=jax.ShapeDtypeStruct((n, D_MODEL), F32),
        compiler_params=_params(("parallel",)),
        name="out_ffn",
    )(x, o_nsa, o_mla, u, u, pw, ps, won, wom, wop, g2, w1, w2, gf)


def _overlap_t(seq):
    ncp = seq // NSA_CMP_STRIDE
    n_slc = seq // NSA_SLC_LEN
    start = np.arange(ncp - 1)[None, :] * NSA_CMP_STRIDE
    end = start + NSA_CMP_LEN
    s0 = np.arange(n_slc)[:, None] * NSA_SLC_LEN
    s1 = s0 + NSA_SLC_LEN
    ov = np.clip(np.minimum(end, s1) - np.maximum(start, s0), 0, None) / NSA_CMP_LEN
    out = np.zeros((LANES, ncp), np.float32)
    out[:n_slc, :ncp - 1] = ov
    return jnp.asarray(out, BF16)


def kernel(x, positions, ln1_g, w_in, nsa_cmp_pos, nsa_cmp_w1, nsa_cmp_w2, mla_q_norm, mla_w_qup, mla_kv_norm,
           mla_w_kvup, pool_w, pool_scale, w_out, ln2_g, w_ff1, w_ff2, final_g):
    batch, seq, _ = x.shape
    depth = w_in.shape[0]
    n = batch * seq
    assert n % min(ROPE_TILE, n) == 0 and n % ROW_TILE == 0
    assert seq % ROW_TILE == 0 and seq % NSA_TILE == 0 and seq % MLA_TILE == 0
    assert seq // NSA_SLC_LEN <= NSA_DH and NSA_WINDOW % NSA_TILE == 0 and NSA_TILE % NSA_SLC_LEN == 0

    cos_t, sin_t = _rope_tables(positions)
    ovt = _overlap_t(seq)
    in_layout, in_layout_t = _in_layout(), _in_layout_t()
    lq, lqr, lk, lv = _mla_layouts()
    won_layout = _layout([r for g in range(NSA_KV_HEADS)
                          for r in ((g * NSA_HPG * NSA_DH, NSA_HPG * NSA_DH, 1.0), (None, NSA_DH, 0.0))])
    row2 = lambda v: v.reshape(1, -1)

    xf = x.reshape(n, D_MODEL)
    for l in range(depth):
        w_all = _take_cols(w_in[l], in_layout)
        wt_all = _take_cols(w_in[l], in_layout_t).T
        wqt, wqrt = _take_cols(mla_w_qup[l], lq).T, _take_cols(mla_w_qup[l], lqr).T
        wk, wvt = _take_cols(mla_w_kvup[l], lk), _take_cols(mla_w_kvup[l], lv).T
        q6t, vst, vwt, gate_t, kvc, kvs, kvw, mqt, mk, mvt, u = _proj_in(
            xf, row2(ln1_g[l]), w_all, wt_all, cos_t, sin_t, row2(mla_q_norm[l]), row2(mla_kv_norm[l]),
            wqt, wqrt, wk, wvt, seq)

        w1c = _blockdiag2(nsa_cmp_w1[l, 0], nsa_cmp_w1[l, 1]).astype(BF16)
        w2c = _blockdiag2(nsa_cmp_w2[l, 0], nsa_cmp_w2[l, 1]).astype(BF16)
        pos = jnp.concatenate([nsa_cmp_pos[l, 0], nsa_cmp_pos[l, 1]], axis=-1)[:, None, :]
        half = NSA_CMP_STRIDE
        kvcmp, vct = _compress(kvc, pos[:half], pos[half:], w1c[:half], w1c[half:], w2c, batch, seq)

        o_nsa = _nsa_attention(q6t, kvcmp, vct, kvs, vst, kvw, vwt, gate_t, ovt, batch, seq)
        o_mla = _mla_attention(mqt, mk, mvt, batch, seq)

        pw = _blockdiag2(_blockdiag2(pool_w[l, 0], pool_w[l, 1]), _blockdiag2(pool_w[l, 2], pool_w[l, 3])).astype(BF16)
        won = _take_rows(w_out[l], won_layout)
        wom = w_out[l, 384:768].astype(BF16)
        wop = w_out[l, 768:1024].astype(BF16)
        xf = _out_ffn(xf, o_nsa, o_mla, u, pw, row2(pool_scale[l]), won, wom, wop, row2(ln2_g[l]),
                      w_ff1[l].astype(BF16), w_ff2[l].astype(BF16), row2(final_g), seq, final=(l == depth - 1))
    return xf.reshape(batch, seq, D_MODEL)
```

```python
import functools

import numpy as np
import jax
import jax.numpy as jnp
from jax import lax
from jax.experimental import pallas as pl
from jax.experimental.pallas import tpu as pltpu

F32 = jnp.float32
BF16 = jnp.bfloat16

D_MODEL = 1024
NSA_HEADS = 6
NSA_KV_HEADS = 2
NSA_HPG = 3
NSA_DH = 64
NSA_CMP_LEN = 32
NSA_CMP_STRIDE = 16
NSA_SLC_LEN = 64
NSA_TOPN = 16
NSA_WINDOW = 512
NSA_FORCE = 1.0e4
MLA_HEADS = 6
MLA_NOPE = 64
MLA_ROPE = 32
MLA_VDIM = 64
V_ROWS = 80
ROPE_THETA = 10000.0
POOL_WIDTH = 256
POOL_HALO = 16
D_FF = 4096
EPS = 1e-6

LANES = 128
SUBLANES = 8
VMEM_LIMIT = 56 * 1024 * 1024

MASKED = -1.0e30
M_FLOOR = -1.0e20
LOG2E = 1.4426950408889634

ROW_TILE = 512
ROPE_TILE = 4096
NSA_TILE = 256
MLA_TILE = 512
FF_CHUNK = 1024
SELECT_VARIANTS = 4


def _dot(a, b):
    return jnp.dot(a, b, preferred_element_type=F32)


def _dot_nt(a, b):
    return lax.dot_general(a, b, (((1,), (1,)), ((), ())), preferred_element_type=F32)


def _rms(x, g):
    return x * lax.rsqrt(jnp.mean(x * x, axis=-1, keepdims=True) + EPS) * g


def _layout(spec):
    return tuple(spec)


def _nkv_col(br, kvi, g):
    return 384 + br * 256 + kvi * 128 + g * 64


def _in_layout():
    spec = []
    for br in range(3):
        for g in range(2):
            spec += [(_nkv_col(br, 0, g), 64, 1.0)]
            spec += [(None, 64, 0.0)] if br == 1 else [(_nkv_col(br, 1, g), 64, 1.0)]
    spec += [(1170, 256, 1.0), (1426, 128, 1.0)]
    spec += [(1586, 256, 1.0)]
    return _layout(spec)


def _in_layout_t():
    spec = []
    spec += [(0, NSA_HEADS * NSA_DH, 0.125)]
    for br in (1, 2):
        for g in range(2):
            spec += [(_nkv_col(br, 1, g), 64, 1.0), (None, V_ROWS - 64, 0.0)]
    for g in range(2):
        spec += [(1152 + g * 9, 9, 1.0), (None, LANES - 9, 0.0)]
    spec += [(None, 64, 0.0), (1554, 32, 1.0), (None, 32, 0.0)]
    spec += [(None, 64, 0.0), (1570, 16, -1.0), (1554, 16, 1.0), (None, 32, 0.0)]
    return _layout(spec)


def _mla_layouts():
    q, qr, k, v = [], [], [], []
    for h in range(MLA_HEADS):
        b = h * 96
        q += [(b, 96, 1.0), (None, 32, 0.0)]
        qr += [(None, 64, 0.0), (b + 80, 16, -1.0), (b + 64, 16, 1.0), (None, 32, 0.0)]
        k += [(h * 128, 64, 1.0), (None, 64, 0.0)]
        v += [(h * 128 + 64, 64, 1.0), (None, V_ROWS - 64, 0.0)]
    return _layout(q), _layout(qr), _layout(k), _layout(v)


def _take_cols(w, layout):
    rows = w.shape[0]
    pieces = [jnp.zeros((rows, n), w.dtype) if start is None else
              (w[:, start:start + n] if scale == 1.0 else w[:, start:start + n] * scale)
              for start, n, scale in layout]
    return jnp.concatenate(pieces, axis=1).astype(BF16)


def _take_rows(w, layout):
    return _take_cols(w.T, layout).T


def _blockdiag2(a, b):
    z = jnp.zeros_like(a)
    top = jnp.concatenate([a, z], axis=-1)
    bot = jnp.concatenate([z, b], axis=-1)
    return jnp.concatenate([top, bot], axis=-2)


def _const_spec(shape):
    nd = len(shape)
    return pl.BlockSpec(shape, lambda *_: (0,) * nd, pipeline_mode=pl.Buffered(1))


def _params(sem):
    return pltpu.CompilerParams(dimension_semantics=sem, vmem_limit_bytes=VMEM_LIMIT)


def _rope_kernel(pos_ref, inv_ref, cos_ref, sin_ref):
    ang = inv_ref[...] * pos_ref[...].astype(F32)
    cos_ref[...] = jnp.cos(ang)
    sin_ref[...] = jnp.sin(ang)


def _rope_tables(positions):
    n = positions.size
    half = MLA_ROPE // 2
    inv = jnp.power(jnp.float32(ROPE_THETA), -(jnp.arange(half, dtype=F32) / half))
    tm = min(ROPE_TILE, n)
    out = jax.ShapeDtypeStruct((half, n), F32)
    return pl.pallas_call(
        _rope_kernel,
        grid=(n // tm,),
        in_specs=[pl.BlockSpec((1, tm), lambda i: (0, i)), _const_spec((half, 1))],
        out_specs=[pl.BlockSpec((half, tm), lambda i: (0, i))] * 2,
        out_shape=[out, out],
        compiler_params=_params(("parallel",)),
        name="rope_tables",
    )(positions.reshape(1, n), inv.reshape(half, 1))


def _ones_rows(rows, cols):
    r = lax.broadcasted_iota(jnp.int32, (rows, cols), 0) % V_ROWS
    return jnp.where(r >= 64, 1.0, 0.0)


def _proj_in_kernel(x_ref, g_ref, w_ref, wt_ref, cost_ref, sint_ref, qg_ref, kvg_ref,
                    wqt_ref, wqrt_ref, wk_ref, wvt_ref,
                    q6t_ref, vst_ref, vwt_ref, gatet_ref, kvc_ref, kvs_ref, kvw_ref, mqt_ref, mk_ref, mvt_ref, u_ref,
                    *, seq):
    h = _rms(x_ref[...], g_ref[...]).astype(BF16)
    zt = _dot_nt(wt_ref[...], h)
    tm = h.shape[0]
    nv = NSA_KV_HEADS * V_ROWS
    nq = NSA_HEADS * NSA_DH
    q6t_ref[...] = (zt[0:nq] * LOG2E).astype(BF16)
    vst_ref[...] = (zt[nq:nq + nv] + _ones_rows(nv, tm)).astype(BF16)
    vwt_ref[...] = (zt[nq + nv:nq + 2 * nv] + _ones_rows(nv, tm)).astype(BF16)
    gatet_ref[...] = jax.nn.sigmoid(zt[nq + 2 * nv:nq + 2 * nv + 256])
    kr_t = zt[nq + 2 * nv + 256:nq + 2 * nv + 384]
    krrot_t = zt[nq + 2 * nv + 384:nq + 2 * nv + 512]
    z = _dot(h, w_ref[:, 0:768])
    kvc_ref[...] = z[:, 0:256]
    tok = (pl.program_id(0) * tm) % seq + lax.broadcasted_iota(jnp.int32, (tm, 2 * LANES), 0)
    lane = lax.broadcasted_iota(jnp.int32, (tm, 2 * LANES), 1) & (LANES - 1)
    onehot = jnp.where(lane - NSA_DH == tok // NSA_SLC_LEN, 1.0, 0.0)
    kvs_ref[...] = (z[:, 256:512] + onehot).astype(BF16)
    kvw_ref[...] = z[:, 512:768].astype(BF16)
    z = _dot(h, w_ref[:, 768:1408])
    u_ref[...] = z[:, 384:640]
    cqn = _rms(z[:, 0:256], qg_ref[...]).astype(BF16)
    qa = _dot_nt(wqt_ref[...], cqn)
    qb = _dot_nt(wqrt_ref[...], cqn)
    c16, s16 = cost_ref[...], sint_ref[...]
    cos_t = jnp.concatenate([jnp.ones((64, tm), F32), c16, c16, jnp.ones((32, tm), F32)], axis=0)
    sin_t = jnp.concatenate([jnp.zeros((64, tm), F32), s16, s16, jnp.zeros((32, tm), F32)], axis=0)
    scale = (MLA_NOPE + MLA_ROPE) ** -0.5 * LOG2E
    for hh in range(MLA_HEADS):
        sl = slice(hh * LANES, (hh + 1) * LANES)
        mqt_ref[sl, :] = ((qa[sl] * cos_t + qb[sl] * sin_t) * scale).astype(BF16)
    ckvn = _rms(z[:, 256:384], kvg_ref[...]).astype(BF16)
    k_rope = (kr_t * cos_t + krrot_t * sin_t).T
    ka = _dot(ckvn, wk_ref[...])
    for hh in range(MLA_HEADS):
        sl = slice(hh * LANES, (hh + 1) * LANES)
        mk_ref[:, sl] = (ka[:, sl] + k_rope).astype(BF16)
    mvt_ref[...] = (_dot_nt(wvt_ref[...], ckvn) + _ones_rows(MLA_HEADS * V_ROWS, tm)).astype(BF16)


def _proj_in(x, g, w_all, wt_all, cos_t, sin_t, qg, kvg, wqt, wqrt, wk, wvt, seq):
    n = x.shape[0]
    tm = ROW_TILE
    row = lambda width: pl.BlockSpec((tm, width), lambda i: (i, 0))
    col = lambda height: pl.BlockSpec((height, tm), lambda i: (0, i))
    nv = NSA_KV_HEADS * V_ROWS
    outs = [("t", NSA_HEADS * NSA_DH, BF16), ("t", nv, BF16), ("t", nv, BF16), ("t", 256, F32), ("r", 256, F32),
            ("r", 256, BF16), ("r", 256, BF16), ("t", 768, BF16), ("r", 768, BF16), ("t", MLA_HEADS * V_ROWS, BF16),
            ("r", 256, F32)]
    return pl.pallas_call(
        functools.partial(_proj_in_kernel, seq=seq),
        grid=(n // tm,),
        in_specs=[row(D_MODEL), _const_spec(g.shape), _const_spec(w_all.shape), _const_spec(wt_all.shape),
                  col(MLA_ROPE // 2), col(MLA_ROPE // 2), _const_spec(qg.shape), _const_spec(kvg.shape),
                  _const_spec(wqt.shape), _const_spec(wqrt.shape), _const_spec(wk.shape), _const_spec(wvt.shape)],
        out_specs=[col(w) if kind == "t" else row(w) for kind, w, _ in outs],
        out_shape=[jax.ShapeDtypeStruct((w, n) if kind == "t" else (n, w), dt) for kind, w, dt in outs],
        compiler_params=_params(("parallel",)),
        name="proj_in",
    )(x, g, w_all, wt_all, cos_t, sin_t, qg, kvg, wqt, wqrt, wk, wvt)


def _compress_kernel(kvc_ref, posa_ref, posb_ref, wa_ref, wb_ref, w2_ref, o_ref, vt_ref, *, ncp):
    a = jnp.zeros((ncp, LANES), F32)
    b = jnp.zeros((ncp, LANES), F32)
    for l in range(NSA_CMP_STRIDE):
        xl = kvc_ref[pl.ds(l, ncp, stride=NSA_CMP_STRIDE), :]
        a = a + _dot((xl + posa_ref[l]).astype(BF16), wa_ref[l])
        b = b + _dot((xl + posb_ref[l]).astype(BF16), wb_ref[l])
    pre = a + pltpu.roll(b, ncp - 1, axis=0)
    hid = jax.nn.gelu(pre).astype(BF16)
    out = _dot(hid, w2_ref[...])
    o_ref[...] = out.astype(BF16)
    ones = jnp.ones((V_ROWS - NSA_DH, ncp), F32)
    vt_ref[...] = jnp.concatenate([out.T[NSA_DH:2 * NSA_DH], ones], axis=0).astype(BF16)


def _compress(kvc, posa, posb, wa, wb, w2, batch, seq):
    ncp = seq // NSA_CMP_STRIDE
    return pl.pallas_call(
        functools.partial(_compress_kernel, ncp=ncp),
        grid=(batch, NSA_KV_HEADS),
        in_specs=[pl.BlockSpec((seq, LANES), lambda b, g: (b, g)), _const_spec(posa.shape), _const_spec(posb.shape),
                  _const_spec(wa.shape), _const_spec(wb.shape), _const_spec(w2.shape)],
        out_specs=[pl.BlockSpec((ncp, LANES), lambda b, g: (b, g)),
                   pl.BlockSpec((V_ROWS, ncp), lambda b, g: (b * NSA_KV_HEADS + g, 0))],
        out_shape=[jax.ShapeDtypeStruct((batch * ncp, 2 * LANES), BF16),
                   jax.ShapeDtypeStruct((batch * NSA_KV_HEADS * V_ROWS, ncp), BF16)],
        compiler_params=_params(("parallel", "parallel")),
        name="nsa_compress",
    )(kvc, posa, posb, wa, wb, w2)


def _flash_update(buf, v_t, m, acc_ref, mask=None):
    load = (lambda: buf[...]) if mask is None else (lambda: jnp.where(mask, buf[...], MASKED))
    m_new = jnp.maximum(m, jnp.max(load(), axis=0, keepdims=True))
    alpha = jnp.exp2(m - m_new)
    p = jnp.exp2(load() - m_new)
    acc_ref[...] = alpha * acc_ref[...] + _dot(v_t, p.astype(BF16))
    return m_new


class _FlashChain:
    def __init__(self, first, last, scores, values, bufs, acc_ref, width, unroll=2):
        self.first, self.last, self.scores, self.values = first, last, scores, values
        self.bufs, self.acc_ref, self.width, self.unroll = bufs, acc_ref, width, unroll
        self.m = None

    def start(self, first_mask=None):
        s_first = self.scores(self.first)
        self.bufs[0][...] = s_first if first_mask is None else jnp.where(first_mask, s_first, MASKED)
        self.acc_ref[...] = jnp.zeros(self.acc_ref.shape, F32)
        self.m = jnp.full((1, self.width), M_FLOOR, F32)

    def loops(self):
        bufs, acc_ref, scores, values = self.bufs, self.acc_ref, self.scores, self.values

        def steps(count, t0):
            def body(jj, m):
                for k in range(count):
                    t = t0 + count * jj + k
                    cur = bufs[k % 2]
                    if count > 1:
                        bufs[(k + 1) % 2][...] = scores(t + 1)
                    m = _flash_update(cur, values(t), m, acc_ref)
                    if count == 1:
                        cur[...] = scores(t + 1)
                return m
            return body

        t0, rem, level = self.first, self.last - self.first, self.unroll
        while level >= 1:
            trips = rem // level
            self.m = lax.fori_loop(0, trips, steps(level, t0), self.m)
            t0, rem, level = t0 + trips * level, rem - trips * level, level // 2

    def finish(self, last_mask):
        _flash_update(self.bufs[0], self.values(self.last), self.m, self.acc_ref, mask=last_mask)
        return self.acc_ref[0:64] * (1.0 / self.acc_ref[64:65])


def _nsa_select(q3, q0, kvc_ref, vct_ref, ovt_ref, ocmp_ref, bias_ref, *, rows, ngroups, tq, n_slc, n_sel):
    width = NSA_HPG * tq
    ncp = kvc_ref.shape[0]
    s = _dot(kvc_ref[0:rows, :], q3)
    nblk = lax.broadcasted_iota(jnp.int32, (rows, width), 0)
    qpos = q0 + (lax.broadcasted_iota(jnp.int32, (1, width), 1) & (tq - 1))
    last_blk = (qpos - (NSA_CMP_LEN - 1)) >> 4
    s = jnp.where(nblk <= last_blk, s, -jnp.inf)
    m = jnp.max(s, axis=0, keepdims=True)
    m = jnp.where(m == -jnp.inf, 0.0, m)
    e = jnp.exp2(s - m)
    d = jnp.sum(e, axis=0, keepdims=True)
    p = e * (1.0 / jnp.where(d > 0.0, d, 1.0))

    def pad(a):
        return a if rows == ncp else jnp.concatenate([a, jnp.zeros((ncp - rows, a.shape[1]), a.dtype)], axis=0)

    ocmp_ref[...] = _dot(vct_ref[0:NSA_DH, :], pad(p.astype(BF16)))

    every_valid_block_selected = q0 + tq <= n_sel * NSA_SLC_LEN

    @pl.when(every_valid_block_selected)
    def _():
        bias_ref[...] = jnp.zeros(bias_ref.shape, BF16)

    @pl.when(jnp.logical_not(every_valid_block_selected))
    def _():
        psum = p[:, 0:tq] + p[:, tq:2 * tq] + p[:, 2 * tq:3 * tq]
        hi = psum.astype(BF16)
        lo = (psum - hi.astype(F32)).astype(BF16)
        ovt = ovt_ref[...]
        imp = _dot(ovt, pad(hi)) + _dot(ovt, pad(lo))
        nblocks = ngroups * SUBLANES
        blk = lax.broadcasted_iota(jnp.int32, (LANES, tq), 0)
        qp = q0 + lax.broadcasted_iota(jnp.int32, (LANES, tq), 1)
        cur = qp >> 6
        forced = (blk == 0) | (blk == cur) | (blk == cur - 1)
        valid = (blk << 6) <= qp
        score = jnp.where(forced, NSA_FORCE, jnp.where(valid, imp, -1.0))[0:nblocks]
        sub = SUBLANES
        sidx = lax.broadcasted_iota(jnp.int32, (sub, tq), 0)
        groups = [score[k * sub:(k + 1) * sub] for k in range(ngroups)]
        cnts = [jnp.zeros((sub, tq), F32) for _ in groups]
        for sp in range(nblocks):
            col = score[sp:sp + 1, :]
            for k, grp in enumerate(groups):
                if k * sub > sp:
                    cnts[k] = cnts[k] + jnp.where(col >= grp, 1.0, 0.0)
                elif (k + 1) * sub - 1 <= sp:
                    cnts[k] = cnts[k] + jnp.where(col > grp, 1.0, 0.0)
                else:
                    after = jnp.where(col >= grp, 1.0, 0.0)
                    cnts[k] = cnts[k] + jnp.where(sidx + k * sub > sp, after, jnp.where(col > grp, 1.0, 0.0))
        bias = jnp.where(jnp.concatenate(cnts, axis=0) < n_sel, 0.0, MASKED)
        if nblocks < NSA_DH:
            bias = jnp.concatenate([bias, jnp.zeros((NSA_DH - nblocks, tq), F32)], axis=0)
        bias_ref[...] = bias.astype(BF16)


def _nsa_kernel(qt_ref, kvc_ref, vct_ref, kvs_ref, vst_ref, kvw_ref, vwt_ref, gt_ref, ovt_ref, o_ref,
                bias_sc, ocmp_sc, acc_sc, sa_sc, sb_sc, wacc_sc, wa_sc, wb_sc, *, tq, n_slc, n_sel):
    nq = qt_ref.shape[1] // tq

    def q_tile(i, carry):
        _nsa_q_tile(i, qt_ref, kvc_ref, vct_ref, kvs_ref, vst_ref, kvw_ref, vwt_ref, gt_ref, ovt_ref, o_ref,
                    bias_sc, ocmp_sc, acc_sc, sa_sc, sb_sc, wacc_sc, wa_sc, wb_sc, tq=tq, n_slc=n_slc, n_sel=n_sel)
        return carry

    lax.fori_loop(0, nq, q_tile, 0)


def _nsa_q_tile(i, qt_ref, kvc_ref, vct_ref, kvs_ref, vst_ref, kvw_ref, vwt_ref, gt_ref, ovt_ref, o_ref,
                bias_sc, ocmp_sc, acc_sc, sa_sc, sb_sc, wacc_sc, wa_sc, wb_sc, *, tq, n_slc, n_sel):
    q0 = pl.multiple_of(i * tq, tq)
    width = NSA_HPG * tq
    q_top = jnp.concatenate([qt_ref[h * NSA_DH:(h + 1) * NSA_DH, pl.ds(q0, tq)] for h in range(NSA_HPG)], axis=1)
    q3 = jnp.concatenate([q_top, jnp.zeros((NSA_DH, width), BF16)], axis=0)

    ncp = kvc_ref.shape[0]
    nq = ncp * NSA_CMP_STRIDE // tq
    divisible = all(v % SELECT_VARIANTS == 0 for v in (nq, ncp // 16, n_slc // SUBLANES))
    nvar = SELECT_VARIANTS if divisible else 1
    for c in range(nvar):
        @pl.when((i * nvar) // nq == c)
        def _(c=c):
            _nsa_select(q3, q0, kvc_ref, vct_ref, ovt_ref, ocmp_sc, bias_sc, rows=ncp * (c + 1) // nvar,
                        ngroups=(n_slc // SUBLANES) * (c + 1) // nvar, tq=tq, n_slc=n_slc, n_sel=n_sel)
    o_cmp = ocmp_sc[...]

    cc = lax.broadcasted_iota(jnp.int32, (tq, width), 0)
    rr = lax.broadcasted_iota(jnp.int32, (tq, width), 1) & (tq - 1)

    bias = bias_sc[...]
    q3_slc = jnp.concatenate([q_top, jnp.concatenate([bias] * NSA_HPG, axis=1)], axis=0)

    def slc_scores(t):
        return _dot(kvs_ref[pl.ds(pl.multiple_of(t * tq, tq), tq), :], q3_slc)

    def slc_values(t):
        return vst_ref[:, pl.ds(pl.multiple_of(t * tq, tq), tq)]

    def win_scores(t):
        return _dot(kvw_ref[pl.ds(pl.multiple_of(t * tq, tq), tq), :], q3)

    def win_values(t):
        return vwt_ref[:, pl.ds(pl.multiple_of(t * tq, tq), tq)]

    back = NSA_WINDOW // tq
    slc = _FlashChain(0, i, slc_scores, slc_values, (sa_sc, sb_sc), acc_sc, width, unroll=4)
    win = _FlashChain(jnp.maximum(i - back, 0), i, win_scores, win_values, (wa_sc, wb_sc), wacc_sc, width)
    slc.start()
    win.start(first_mask=jnp.logical_or(cc > rr, i < back))
    slc.loops()
    win.loops()
    o_slc = slc.finish(cc <= rr)
    o_win = win.finish(cc <= rr)

    gt = gt_ref[:, pl.ds(q0, tq)]
    outs = []
    for h in range(NSA_HPG):
        sl = slice(h * tq, (h + 1) * tq)
        outs.append(gt[3 * h:3 * h + 1] * o_cmp[:, sl] + gt[3 * h + 1:3 * h + 2] * o_slc[:, sl]
                    + gt[3 * h + 2:3 * h + 3] * o_win[:, sl])
    outs.append(jnp.zeros((NSA_DH, tq), F32))
    o_ref[pl.ds(q0, tq), :] = jnp.concatenate(outs, axis=0).T.astype(BF16)


def _nsa_attention(q6t, kvcmp, vct, kvs, vst, kvw, vwt, gate_t, ovt, batch, seq):
    tq = NSA_TILE
    ncp = seq // NSA_CMP_STRIDE
    n_slc = seq // NSA_SLC_LEN
    width = NSA_HPG * tq
    kern = functools.partial(_nsa_kernel, tq=tq, n_slc=n_slc, n_sel=min(NSA_TOPN, n_slc))
    kv_spec = pl.BlockSpec((seq, LANES), lambda b, g: (b, g))
    vt_spec = pl.BlockSpec((V_ROWS, seq), lambda b, g: (g, b))
    return pl.pallas_call(
        kern,
        grid=(batch, NSA_KV_HEADS),
        in_specs=[pl.BlockSpec((NSA_HPG * NSA_DH, seq), lambda b, g: (g, b)),
                  pl.BlockSpec((ncp, LANES), lambda b, g: (b, g)),
                  pl.BlockSpec((V_ROWS, ncp), lambda b, g: (b * NSA_KV_HEADS + g, 0)),
                  kv_spec, vt_spec, kv_spec, vt_spec,
                  pl.BlockSpec((LANES, seq), lambda b, g: (g, b)),
                  _const_spec(ovt.shape)],
        out_specs=pl.BlockSpec((seq, 2 * LANES), lambda b, g: (b, g)),
        out_shape=jax.ShapeDtypeStruct((batch * seq, NSA_KV_HEADS * 2 * LANES), BF16),
        scratch_shapes=[pltpu.VMEM((NSA_DH, tq), BF16), pltpu.VMEM((NSA_DH, width), F32),
                        pltpu.VMEM((V_ROWS, width), F32), pltpu.VMEM((tq, width), F32), pltpu.VMEM((tq, width), F32),
                        pltpu.VMEM((V_ROWS, width), F32), pltpu.VMEM((tq, width), F32), pltpu.VMEM((tq, width), F32)],
        compiler_params=_params(("parallel", "parallel")),
        name="nsa_attention",
    )(q6t, kvcmp, vct, kvs, vst, kvw, vwt, gate_t, ovt)


def _mla_kernel(qt_ref, k_ref, vt_ref, o_ref, acc0_sc, sa0_sc, sb0_sc, acc1_sc, sa1_sc, sb1_sc, *, tq):
    nq = qt_ref.shape[1] // tq
    causal = lax.broadcasted_iota(jnp.int32, (tq, tq), 0) <= lax.broadcasted_iota(jnp.int32, (tq, tq), 1)

    def q_tile(i, carry):
        q0 = pl.multiple_of(i * tq, tq)
        chains = []
        for hh, acc_sc, bufs in ((0, acc0_sc, (sa0_sc, sb0_sc)), (1, acc1_sc, (sa1_sc, sb1_sc))):
            def scores(t, hh=hh):
                off = pl.multiple_of(t * tq, tq)
                return _dot(k_ref[pl.ds(off, tq), hh * LANES:(hh + 1) * LANES],
                            qt_ref[hh * LANES:(hh + 1) * LANES, pl.ds(q0, tq)])

            def values(t, hh=hh):
                return vt_ref[hh * V_ROWS:(hh + 1) * V_ROWS, pl.ds(pl.multiple_of(t * tq, tq), tq)]

            chains.append(_FlashChain(0, i, scores, values, bufs, acc_sc, tq, unroll=4))
        for chain in chains:
            chain.start()
        for chain in chains:
            chain.loops()
        outs = [chain.finish(causal) for chain in chains]
        o_ref[pl.ds(q0, tq), :] = jnp.concatenate(outs, axis=0).T.astype(BF16)
        return carry

    lax.fori_loop(0, nq, q_tile, 0)


def _mla_attention(mqt, mk, mvt, batch, seq):
    tq = MLA_TILE
    pairs = MLA_HEADS // 2
    return pl.pallas_call(
        functools.partial(_mla_kernel, tq=tq),
        grid=(batch, pairs),
        in_specs=[pl.BlockSpec((2 * LANES, seq), lambda b, p: (p, b)),
                  pl.BlockSpec((seq, 2 * LANES), lambda b, p: (b, p)),
                  pl.BlockSpec((2 * V_ROWS, seq), lambda b, p: (p, b))],
        out_specs=pl.BlockSpec((seq, LANES), lambda b, p: (b, p)),
        out_shape=jax.ShapeDtypeStruct((batch * seq, pairs * LANES), BF16),
        scratch_shapes=[pltpu.VMEM((V_ROWS, tq), F32), pltpu.VMEM((tq, tq), F32), pltpu.VMEM((tq, tq), F32)] * 2,
        compiler_params=_params(("parallel", "parallel")),
        name="mla_attention",
    )(mqt, mk, mvt)


def _out_ffn_kernel(x_ref, on_ref, om_ref, u_ref, uh_ref, pw_ref, ps_ref, won_ref, wom_ref, wop_ref, g2_ref,
                    w1_ref, w2_ref, gf_ref, y_ref, *, tm, seq, final):
    i = pl.program_id(0)
    t0 = (i * tm) % seq
    u = u_ref[...]
    halo = jnp.where(t0 == 0, 0.0, uh_ref[...])
    ext = jnp.concatenate([halo, u], axis=0)
    s2 = ext + pltpu.roll(ext, 1, axis=0)
    s4 = s2 + pltpu.roll(s2, 2, axis=0)
    s8 = s4 + pltpu.roll(s4, 4, axis=0)
    s16 = s8 + pltpu.roll(s8, 8, axis=0)
    sl = slice(POOL_HALO, POOL_HALO + tm)
    lane = lax.broadcasted_iota(jnp.int32, (tm, POOL_WIDTH), 1)
    t = t0 + lax.broadcasted_iota(jnp.int32, (tm, POOL_WIDTH), 0)
    sums = jnp.where(lane < 64, s2[sl], jnp.where(lane < 128, s4[sl], jnp.where(lane < 192, s8[sl], s16[sl])))
    win = jnp.where(lane < 64, 2, jnp.where(lane < 128, 4, jnp.where(lane < 192, 8, 16)))
    cnt = jnp.minimum(t + 1, win).astype(F32)
    pooled = (sums / cnt - u).astype(BF16)
    y_pool = (_dot(pooled, pw_ref[...]) * ps_ref[...]).astype(BF16)
    mix = _dot(on_ref[...], won_ref[...]) + _dot(om_ref[...], wom_ref[...]) + _dot(y_pool, wop_ref[...])
    x = x_ref[...] + mix
    h = _rms(x, g2_ref[...]).astype(BF16)
    ff = None
    for c in range(D_FF // FF_CHUNK):
        cs = slice(c * FF_CHUNK, (c + 1) * FF_CHUNK)
        a = jnp.maximum(_dot(h, w1_ref[:, cs]), 0.0)
        part = _dot((a * a).astype(BF16), w2_ref[cs, :])
        ff = part if ff is None else ff + part
    acc = x + ff
    if final:
        acc = _rms(acc, gf_ref[...])
    y_ref[...] = acc


def _out_ffn(x, o_nsa, o_mla, u, pw, ps, won, wom, wop, g2, w1, w2, gf, seq, final):
    n = x.shape[0]
    tm = ROW_TILE
    hb = tm // POOL_HALO
    row = lambda width: pl.BlockSpec((tm, width), lambda i: (i, 0))
    return pl.pallas_call(
        functools.partial(_out_ffn_kernel, tm=tm, seq=seq, final=final),
        grid=(n // tm,),
        in_specs=[row(D_MODEL), row(o_nsa.shape[1]), row(o_mla.shape[1]), row(POOL_WIDTH),
                  pl.BlockSpec((POOL_HALO, POOL_WIDTH), lambda i: (jnp.maximum(i * hb - 1, 0), 0)),
                  _const_spec(pw.shape), _const_spec(ps.shape), _const_spec(won.shape), _const_spec(wom.shape),
                  _const_spec(wop.shape), _const_spec(g2.shape), _const_spec(w1.shape), _const_spec(w2.shape),
                  _const_spec(gf.shape)],
        out_specs=row(D_MODEL),
        out_shape=jax.ShapeDtypeStruct((n, D_MODEL), F32),
        compiler_params=_params(("parallel",)),
        name="out_ffn",
    )(x, o_nsa, o_mla, u, u, pw, ps, won, wom, wop, g2, w1, w2, gf)


def _overlap_t(seq):
    ncp = seq // NSA_CMP_STRIDE
    n_slc = seq // NSA_SLC_LEN
    start = np.arange(ncp - 1)[None, :] * NSA_CMP_STRIDE
    end = start + NSA_CMP_LEN
    s0 = np.arange(n_slc)[:, None] * NSA_SLC_LEN
    s1 = s0 + NSA_SLC_LEN
    ov = np.clip(np.minimum(end, s1) - np.maximum(start, s0), 0, None) / NSA_CMP_LEN
    out = np.zeros((LANES, ncp), np.float32)
    out[:n_slc, :ncp - 1] = ov
    return jnp.asarray(out, BF16)


def kernel(x, positions, ln1_g, w_in, nsa_cmp_pos, nsa_cmp_w1, nsa_cmp_w2, mla_q_norm, mla_w_qup, mla_kv_norm,
           mla_w_kvup, pool_w, pool_scale, w_out, ln2_g, w_ff1, w_ff2, final_g):
    batch, seq, _ = x.shape
    depth = w_in.shape[0]
    n = batch * seq
    assert n % min(ROPE_TILE, n) == 0 and n % ROW_TILE == 0
    assert seq % ROW_TILE == 0 and seq % NSA_TILE == 0 and seq % MLA_TILE == 0
    assert seq // NSA_SLC_LEN <= NSA_DH and NSA_WINDOW % NSA_TILE == 0 and NSA_TILE % NSA_SLC_LEN == 0

    cos_t, sin_t = _rope_tables(positions)
    ovt = _overlap_t(seq)
    in_layout, in_layout_t = _in_layout(), _in_layout_t()
    lq, lqr, lk, lv = _mla_layouts()
    won_layout = _layout([r for g in range(NSA_KV_HEADS)
                          for r in ((g * NSA_HPG * NSA_DH, NSA_HPG * NSA_DH, 1.0), (None, NSA_DH, 0.0))])
    row2 = lambda v: v.reshape(1, -1)

    xf = x.reshape(n, D_MODEL)
    for l in range(depth):
        w_all = _take_cols(w_in[l], in_layout)
        wt_all = _take_cols(w_in[l], in_layout_t).T
        wqt, wqrt = _take_cols(mla_w_qup[l], lq).T, _take_cols(mla_w_qup[l], lqr).T
        wk, wvt = _take_cols(mla_w_kvup[l], lk), _take_cols(mla_w_kvup[l], lv).T
        q6t, vst, vwt, gate_t, kvc, kvs, kvw, mqt, mk, mvt, u = _proj_in(
            xf, row2(ln1_g[l]), w_all, wt_all, cos_t, sin_t, row2(mla_q_norm[l]), row2(mla_kv_norm[l]),
            wqt, wqrt, wk, wvt, seq)

        w1c = _blockdiag2(nsa_cmp_w1[l, 0], nsa_cmp_w1[l, 1]).astype(BF16)
        w2c = _blockdiag2(nsa_cmp_w2[l, 0], nsa_cmp_w2[l, 1]).astype(BF16)
        pos = jnp.concatenate([nsa_cmp_pos[l, 0], nsa_cmp_pos[l, 1]], axis=-1)[:, None, :]
        half = NSA_CMP_STRIDE
        kvcmp, vct = _compress(kvc, pos[:half], pos[half:], w1c[:half], w1c[half:], w2c, batch, seq)

        o_nsa = _nsa_attention(q6t, kvcmp, vct, kvs, vst, kvw, vwt, gate_t, ovt, batch, seq)
        o_mla = _mla_attention(mqt, mk, mvt, batch, seq)

        pw = _blockdiag2(_blockdiag2(pool_w[l, 0], pool_w[l, 1]), _blockdiag2(pool_w[l, 2], pool_w[l, 3])).astype(BF16)
        won = _take_rows(w_out[l], won_layout)
        wom = w_out[l, 384:768].astype(BF16)
        wop = w_out[l, 768:1024].astype(BF16)
        xf = _out_ffn(xf, o_nsa, o_mla, u, pw, row2(pool_scale[l]), won, wom, wop, row2(ln2_g[l]),
                      w_ff1[l].astype(BF16), w_ff2[l].astype(BF16), row2(final_g), seq, final=(l == depth - 1))
    return xf.reshape(batch, seq, D_MODEL)
```

```python
import functools

import numpy as np
import jax
import jax.numpy as jnp
from jax import lax
from jax.experimental import pallas as pl
from jax.experimental.pallas import tpu as pltpu

F32 = jnp.float32
BF16 = jnp.bfloat16

D_MODEL = 1024
NSA_HEADS = 6
NSA_KV_HEADS = 2
NSA_HPG = 3
NSA_DH = 64
NSA_CMP_LEN = 32
NSA_CMP_STRIDE = 16
NSA_SLC_LEN = 64
NSA_TOPN = 16
NSA_WINDOW = 512
NSA_FORCE = 1.0e4
MLA_HEADS = 6
MLA_NOPE = 64
MLA_ROPE = 32
MLA_VDIM = 64
V_ROWS = 80
ROPE_THETA = 10000.0
POOL_WIDTH = 256
POOL_HALO = 16
D_FF = 4096
EPS = 1e-6

COL_NSA_KV = NSA_HEADS * NSA_DH
COL_GATE = COL_NSA_KV + 3 * 2 * NSA_KV_HEADS * NSA_DH
COL_CQ = COL_GATE + 3 * NSA_HEADS
MLA_Q_LORA = 256
MLA_KV_LORA = 128
COL_CKV = COL_CQ + MLA_Q_LORA
COL_KR = COL_CKV + MLA_KV_LORA
COL_POOL = COL_KR + MLA_ROPE
D_MIX_NSA = NSA_HEADS * NSA_DH
D_MIX_MLA = MLA_HEADS * MLA_VDIM

LANES = 128
SUBLANES = 8
VMEM_LIMIT = 56 * 1024 * 1024

MASKED = -1.0e30
M_FLOOR = -1.0e20
LOG2E = 1.4426950408889634

ROW_TILE = 512
ROPE_TILE = 4096
NSA_TILE = 256
MLA_TILE = 512
FF_CHUNK = 1024
SELECT_VARIANTS = 4


def _dot(a, b):
    return jnp.dot(a, b, preferred_element_type=F32)


def _dot_nt(a, b):
    return lax.dot_general(a, b, (((1,), (1,)), ((), ())), preferred_element_type=F32)


def _rms(x, g):
    return x * lax.rsqrt(jnp.mean(x * x, axis=-1, keepdims=True) + EPS) * g


def _layout(spec):
    return tuple(spec)


def _nkv_col(br, kvi, g):
    return COL_NSA_KV + (br * 2 + kvi) * NSA_KV_HEADS * NSA_DH + g * NSA_DH


def _in_layout():
    spec = []
    for br in range(3):
        for g in range(2):
            spec += [(_nkv_col(br, 0, g), 64, 1.0)]
            spec += [(None, 64, 0.0)] if br == 1 else [(_nkv_col(br, 1, g), 64, 1.0)]
    spec += [(COL_CQ, MLA_Q_LORA, 1.0), (COL_CKV, MLA_KV_LORA, 1.0)]
    spec += [(COL_POOL, POOL_WIDTH, 1.0)]
    return _layout(spec)


def _in_layout_t():
    spec = []
    spec += [(0, NSA_HEADS * NSA_DH, 0.125)]
    for br in (1, 2):
        for g in range(2):
            spec += [(_nkv_col(br, 1, g), 64, 1.0), (None, V_ROWS - 64, 0.0)]
    for g in range(2):
        spec += [(COL_GATE + g * 3 * NSA_HPG, 3 * NSA_HPG, 1.0), (None, LANES - 3 * NSA_HPG, 0.0)]
    half = MLA_ROPE // 2
    spec += [(None, MLA_NOPE, 0.0), (COL_KR, MLA_ROPE, 1.0), (None, LANES - MLA_NOPE - MLA_ROPE, 0.0)]
    spec += [(None, MLA_NOPE, 0.0), (COL_KR + half, half, -1.0), (COL_KR, half, 1.0),
             (None, LANES - MLA_NOPE - MLA_ROPE, 0.0)]
    return _layout(spec)


def _mla_layouts():
    q, qr, k, v = [], [], [], []
    for h in range(MLA_HEADS):
        b = h * 96
        q += [(b, 96, 1.0), (None, 32, 0.0)]
        qr += [(None, 64, 0.0), (b + 80, 16, -1.0), (b + 64, 16, 1.0), (None, 32, 0.0)]
        k += [(h * 128, 64, 1.0), (None, 64, 0.0)]
        v += [(h * 128 + 64, 64, 1.0), (None, V_ROWS - 64, 0.0)]
    return _layout(q), _layout(qr), _layout(k), _layout(v)


def _take_cols(w, layout):
    rows = w.shape[0]
    pieces = [jnp.zeros((rows, n), w.dtype) if start is None else
              (w[:, start:start + n] if scale == 1.0 else w[:, start:start + n] * scale)
              for start, n, scale in layout]
    return jnp.concatenate(pieces, axis=1).astype(BF16)


def _take_rows(w, layout):
    return _take_cols(w.T, layout).T


def _blockdiag2(a, b):
    z = jnp.zeros_like(a)
    top = jnp.concatenate([a, z], axis=-1)
    bot = jnp.concatenate([z, b], axis=-1)
    return jnp.concatenate([top, bot], axis=-2)


def _const_spec(shape):
    nd = len(shape)
    return pl.BlockSpec(shape, lambda *_: (0,) * nd, pipeline_mode=pl.Buffered(1))


def _params(sem):
    return pltpu.CompilerParams(dimension_semantics=sem, vmem_limit_bytes=VMEM_LIMIT)


def _rope_kernel(pos_ref, inv_ref, cos_ref, sin_ref):
    ang = inv_ref[...] * pos_ref[...].astype(F32)
    cos_ref[...] = jnp.cos(ang)
    sin_ref[...] = jnp.sin(ang)


def _rope_tables(positions):
    n = positions.size
    half = MLA_ROPE // 2
    inv = jnp.power(jnp.float32(ROPE_THETA), -(jnp.arange(half, dtype=F32) / half))
    tm = min(ROPE_TILE, n)
    out = jax.ShapeDtypeStruct((half, n), F32)
    return pl.pallas_call(
        _rope_kernel,
        grid=(n // tm,),
        in_specs=[pl.BlockSpec((1, tm), lambda i: (0, i)), _const_spec((half, 1))],
        out_specs=[pl.BlockSpec((half, tm), lambda i: (0, i))] * 2,
        out_shape=[out, out],
        compiler_params=_params(("parallel",)),
        name="rope_tables",
    )(positions.reshape(1, n), inv.reshape(half, 1))


def _ones_rows(rows, cols):
    r = lax.broadcasted_iota(jnp.int32, (rows, cols), 0) % V_ROWS
    return jnp.where(r >= 64, 1.0, 0.0)


def _proj_in_kernel(x_ref, g_ref, w_ref, wt_ref, cost_ref, sint_ref, qg_ref, kvg_ref,
                    wqt_ref, wqrt_ref, wk_ref, wvt_ref,
                    q6t_ref, vst_ref, vwt_ref, gatet_ref, kvc_ref, kvs_ref, kvw_ref, mqt_ref, mk_ref, mvt_ref, u_ref,
                    *, seq):
    h = _rms(x_ref[...], g_ref[...]).astype(BF16)
    zt = _dot_nt(wt_ref[...], h)
    tm = h.shape[0]
    nv = NSA_KV_HEADS * V_ROWS
    nq = NSA_HEADS * NSA_DH
    q6t_ref[...] = (zt[0:nq] * LOG2E).astype(BF16)
    vst_ref[...] = (zt[nq:nq + nv] + _ones_rows(nv, tm)).astype(BF16)
    vwt_ref[...] = (zt[nq + nv:nq + 2 * nv] + _ones_rows(nv, tm)).astype(BF16)
    r0 = nq + 2 * nv
    gatet_ref[...] = jax.nn.sigmoid(zt[r0:r0 + NSA_KV_HEADS * LANES])
    kr_t = zt[r0 + NSA_KV_HEADS * LANES:r0 + (NSA_KV_HEADS + 1) * LANES]
    krrot_t = zt[r0 + (NSA_KV_HEADS + 1) * LANES:r0 + (NSA_KV_HEADS + 2) * LANES]
    kvw = NSA_KV_HEADS * LANES
    z = _dot(h, w_ref[:, 0:3 * kvw])
    kvc_ref[...] = z[:, 0:kvw]
    tok = (pl.program_id(0) * tm) % seq + lax.broadcasted_iota(jnp.int32, (tm, 2 * LANES), 0)
    lane = lax.broadcasted_iota(jnp.int32, (tm, 2 * LANES), 1) & (LANES - 1)
    onehot = jnp.where(lane - NSA_DH == tok // NSA_SLC_LEN, 1.0, 0.0)
    kvs_ref[...] = (z[:, kvw:2 * kvw] + onehot).astype(BF16)
    kvw_ref[...] = z[:, 2 * kvw:3 * kvw].astype(BF16)
    lat = MLA_Q_LORA + MLA_KV_LORA
    z = _dot(h, w_ref[:, 3 * kvw:3 * kvw + lat + POOL_WIDTH])
    u_ref[...] = z[:, lat:lat + POOL_WIDTH]
    cqn = _rms(z[:, 0:MLA_Q_LORA], qg_ref[...]).astype(BF16)
    qa = _dot_nt(wqt_ref[...], cqn)
    qb = _dot_nt(wqrt_ref[...], cqn)
    c16, s16 = cost_ref[...], sint_ref[...]
    cos_t = jnp.concatenate([jnp.ones((64, tm), F32), c16, c16, jnp.ones((32, tm), F32)], axis=0)
    sin_t = jnp.concatenate([jnp.zeros((64, tm), F32), s16, s16, jnp.zeros((32, tm), F32)], axis=0)
    scale = (MLA_NOPE + MLA_ROPE) ** -0.5 * LOG2E
    for hh in range(MLA_HEADS):
        sl = slice(hh * LANES, (hh + 1) * LANES)
        mqt_ref[sl, :] = ((qa[sl] * cos_t + qb[sl] * sin_t) * scale).astype(BF16)
    ckvn = _rms(z[:, MLA_Q_LORA:lat], kvg_ref[...]).astype(BF16)
    k_rope = (kr_t * cos_t + krrot_t * sin_t).T
    ka = _dot(ckvn, wk_ref[...])
    for hh in range(MLA_HEADS):
        sl = slice(hh * LANES, (hh + 1) * LANES)
        mk_ref[:, sl] = (ka[:, sl] + k_rope).astype(BF16)
    mvt_ref[...] = (_dot_nt(wvt_ref[...], ckvn) + _ones_rows(MLA_HEADS * V_ROWS, tm)).astype(BF16)


def _proj_in(x, g, w_all, wt_all, cos_t, sin_t, qg, kvg, wqt, wqrt, wk, wvt, seq):
    n = x.shape[0]
    tm = ROW_TILE
    row = lambda width: pl.BlockSpec((tm, width), lambda i: (i, 0))
    col = lambda height: pl.BlockSpec((height, tm), lambda i: (0, i))
    nv = NSA_KV_HEADS * V_ROWS
    outs = [("t", NSA_HEADS * NSA_DH, BF16), ("t", nv, BF16), ("t", nv, BF16), ("t", 256, F32), ("r", 256, F32),
            ("r", 256, BF16), ("r", 256, BF16), ("t", 768, BF16), ("r", 768, BF16), ("t", MLA_HEADS * V_ROWS, BF16),
            ("r", 256, F32)]
    return pl.pallas_call(
        functools.partial(_proj_in_kernel, seq=seq),
        grid=(n // tm,),
        in_specs=[row(D_MODEL), _const_spec(g.shape), _const_spec(w_all.shape), _const_spec(wt_all.shape),
                  col(MLA_ROPE // 2), col(MLA_ROPE // 2), _const_spec(qg.shape), _const_spec(kvg.shape),
                  _const_spec(wqt.shape), _const_spec(wqrt.shape), _const_spec(wk.shape), _const_spec(wvt.shape)],
        out_specs=[col(w) if kind == "t" else row(w) for kind, w, _ in outs],
        out_shape=[jax.ShapeDtypeStruct((w, n) if kind == "t" else (n, w), dt) for kind, w, dt in outs],
        compiler_params=_params(("parallel",)),
        name="proj_in",
    )(x, g, w_all, wt_all, cos_t, sin_t, qg, kvg, wqt, wqrt, wk, wvt)


def _compress_kernel(kvc_ref, posa_ref, posb_ref, wa_ref, wb_ref, w2_ref, o_ref, vt_ref, *, ncp):
    a = jnp.zeros((ncp, LANES), F32)
    b = jnp.zeros((ncp, LANES), F32)
    for l in range(NSA_CMP_STRIDE):
        xl = kvc_ref[pl.ds(l, ncp, stride=NSA_CMP_STRIDE), :]
        a = a + _dot((xl + posa_ref[l]).astype(BF16), wa_ref[l])
        b = b + _dot((xl + posb_ref[l]).astype(BF16), wb_ref[l])
    pre = a + pltpu.roll(b, ncp - 1, axis=0)
    hid = jax.nn.gelu(pre).astype(BF16)
    out = _dot(hid, w2_ref[...])
    o_ref[...] = out.astype(BF16)
    ones = jnp.ones((V_ROWS - NSA_DH, ncp), F32)
    vt_ref[...] = jnp.concatenate([out.T[NSA_DH:2 * NSA_DH], ones], axis=0).astype(BF16)


def _compress(kvc, posa, posb, wa, wb, w2, batch, seq):
    ncp = seq // NSA_CMP_STRIDE
    return pl.pallas_call(
        functools.partial(_compress_kernel, ncp=ncp),
        grid=(batch, NSA_KV_HEADS),
        in_specs=[pl.BlockSpec((seq, LANES), lambda b, g: (b, g)), _const_spec(posa.shape), _const_spec(posb.shape),
                  _const_spec(wa.shape), _const_spec(wb.shape), _const_spec(w2.shape)],
        out_specs=[pl.BlockSpec((ncp, LANES), lambda b, g: (b, g)),
                   pl.BlockSpec((V_ROWS, ncp), lambda b, g: (b * NSA_KV_HEADS + g, 0))],
        out_shape=[jax.ShapeDtypeStruct((batch * ncp, 2 * LANES), BF16),
                   jax.ShapeDtypeStruct((batch * NSA_KV_HEADS * V_ROWS, ncp), BF16)],
        compiler_params=_params(("parallel", "parallel")),
        name="nsa_compress",
    )(kvc, posa, posb, wa, wb, w2)


def _flash_update(buf, v_t, m, acc_ref, mask=None):
    load = (lambda: buf[...]) if mask is None else (lambda: jnp.where(mask, buf[...], MASKED))
    m_new = jnp.maximum(m, jnp.max(load(), axis=0, keepdims=True))
    alpha = jnp.exp2(m - m_new)
    p = jnp.exp2(load() - m_new)
    acc_ref[...] = alpha * acc_ref[...] + _dot(v_t, p.astype(BF16))
    return m_new


class _FlashChain:
    def __init__(self, first, last, scores, values, bufs, acc_ref, width, unroll=2):
        self.first, self.last, self.scores, self.values = first, last, scores, values
        self.bufs, self.acc_ref, self.width, self.unroll = bufs, acc_ref, width, unroll
        self.m = None

    def start(self, first_mask=None):
        s_first = self.scores(self.first)
        self.bufs[0][...] = s_first if first_mask is None else jnp.where(first_mask, s_first, MASKED)
        self.acc_ref[...] = jnp.zeros(self.acc_ref.shape, F32)
        self.m = jnp.full((1, self.width), M_FLOOR, F32)

    def loops(self):
        bufs, acc_ref, scores, values = self.bufs, self.acc_ref, self.scores, self.values

        def steps(count, t0):
            def body(jj, m):
                for k in range(count):
                    t = t0 + count * jj + k
                    cur = bufs[k % 2]
                    if count > 1:
                        bufs[(k + 1) % 2][...] = scores(t + 1)
                    m = _flash_update(cur, values(t), m, acc_ref)
                    if count == 1:
                        cur[...] = scores(t + 1)
                return m
            return body

        t0, rem, level = self.first, self.last - self.first, self.unroll
        while level >= 1:
            trips = rem // level
            self.m = lax.fori_loop(0, trips, steps(level, t0), self.m)
            t0, rem, level = t0 + trips * level, rem - trips * level, level // 2

    def finish(self, last_mask):
        _flash_update(self.bufs[0], self.values(self.last), self.m, self.acc_ref, mask=last_mask)
        return self.acc_ref[0:64] * (1.0 / self.acc_ref[64:65])


def _nsa_select(q3, q0, kvc_ref, vct_ref, ovt_ref, ocmp_ref, bias_ref, *, rows, ngroups, tq, n_slc, n_sel):
    width = NSA_HPG * tq
    ncp = kvc_ref.shape[0]
    s = _dot(kvc_ref[0:rows, :], q3)
    nblk = lax.broadcasted_iota(jnp.int32, (rows, width), 0)
    qpos = q0 + (lax.broadcasted_iota(jnp.int32, (1, width), 1) & (tq - 1))
    last_blk = (qpos - (NSA_CMP_LEN - 1)) >> 4
    s = jnp.where(nblk <= last_blk, s, -jnp.inf)
    m = jnp.max(s, axis=0, keepdims=True)
    m = jnp.where(m == -jnp.inf, 0.0, m)
    e = jnp.exp2(s - m)
    d = jnp.sum(e, axis=0, keepdims=True)
    p = e * (1.0 / jnp.where(d > 0.0, d, 1.0))

    def pad(a):
        return a if rows == ncp else jnp.concatenate([a, jnp.zeros((ncp - rows, a.shape[1]), a.dtype)], axis=0)

    ocmp_ref[...] = _dot(vct_ref[0:NSA_DH, :], pad(p.astype(BF16)))

    every_valid_block_selected = q0 + tq <= n_sel * NSA_SLC_LEN

    @pl.when(every_valid_block_selected)
    def _():
        bias_ref[...] = jnp.zeros(bias_ref.shape, BF16)

    @pl.when(jnp.logical_not(every_valid_block_selected))
    def _():
        psum = p[:, 0:tq] + p[:, tq:2 * tq] + p[:, 2 * tq:3 * tq]
        hi = psum.astype(BF16)
        lo = (psum - hi.astype(F32)).astype(BF16)
        ovt = ovt_ref[...]
        imp = _dot(ovt, pad(hi)) + _dot(ovt, pad(lo))
        nblocks = ngroups * SUBLANES
        blk = lax.broadcasted_iota(jnp.int32, (LANES, tq), 0)
        qp = q0 + lax.broadcasted_iota(jnp.int32, (LANES, tq), 1)
        cur = qp >> 6
        forced = (blk == 0) | (blk == cur) | (blk == cur - 1)
        valid = (blk << 6) <= qp
        score = jnp.where(forced, NSA_FORCE, jnp.where(valid, imp, -1.0))[0:nblocks]
        sub = SUBLANES
        sidx = lax.broadcasted_iota(jnp.int32, (sub, tq), 0)
        groups = [score[k * sub:(k + 1) * sub] for k in range(ngroups)]
        cnts = [jnp.zeros((sub, tq), F32) for _ in groups]
        for sp in range(nblocks):
            col = score[sp:sp + 1, :]
            for k, grp in enumerate(groups):
                if k * sub > sp:
                    cnts[k] = cnts[k] + jnp.where(col >= grp, 1.0, 0.0)
                elif (k + 1) * sub - 1 <= sp:
                    cnts[k] = cnts[k] + jnp.where(col > grp, 1.0, 0.0)
                else:
                    after = jnp.where(col >= grp, 1.0, 0.0)
                    cnts[k] = cnts[k] + jnp.where(sidx + k * sub > sp, after, jnp.where(col > grp, 1.0, 0.0))
        bias = jnp.where(jnp.concatenate(cnts, axis=0) < n_sel, 0.0, MASKED)
        if nblocks < NSA_DH:
            bias = jnp.concatenate([bias, jnp.zeros((NSA_DH - nblocks, tq), F32)], axis=0)
        bias_ref[...] = bias.astype(BF16)


def _nsa_kernel(qt_ref, kvc_ref, vct_ref, kvs_ref, vst_ref, kvw_ref, vwt_ref, gt_ref, ovt_ref, o_ref,
                bias_sc, ocmp_sc, acc_sc, sa_sc, sb_sc, wacc_sc, wa_sc, wb_sc, *, tq, n_slc, n_sel):
    nq = qt_ref.shape[1] // tq

    def q_tile(i, carry):
        _nsa_q_tile(i, qt_ref, kvc_ref, vct_ref, kvs_ref, vst_ref, kvw_ref, vwt_ref, gt_ref, ovt_ref, o_ref,
                    bias_sc, ocmp_sc, acc_sc, sa_sc, sb_sc, wacc_sc, wa_sc, wb_sc, tq=tq, n_slc=n_slc, n_sel=n_sel)
        return carry

    lax.fori_loop(0, nq, q_tile, 0)


def _nsa_q_tile(i, qt_ref, kvc_ref, vct_ref, kvs_ref, vst_ref, kvw_ref, vwt_ref, gt_ref, ovt_ref, o_ref,
                bias_sc, ocmp_sc, acc_sc, sa_sc, sb_sc, wacc_sc, wa_sc, wb_sc, *, tq, n_slc, n_sel):
    q0 = pl.multiple_of(i * tq, tq)
    width = NSA_HPG * tq
    q_top = jnp.concatenate([qt_ref[h * NSA_DH:(h + 1) * NSA_DH, pl.ds(q0, tq)] for h in range(NSA_HPG)],
                            axis=1)
    q3 = jnp.concatenate([q_top, jnp.zeros((NSA_DH, width), BF16)], axis=0)

    ncp = kvc_ref.shape[0]
    nq = ncp * NSA_CMP_STRIDE // tq
    divisible = all(v % SELECT_VARIANTS == 0 for v in (nq, ncp // 16, n_slc // SUBLANES))
    nvar = SELECT_VARIANTS if divisible else 1
    for c in range(nvar):
        @pl.when((i * nvar) // nq == c)
        def _(c=c):
            _nsa_select(q3, q0, kvc_ref, vct_ref, ovt_ref, ocmp_sc, bias_sc, rows=ncp * (c + 1) // nvar,
                        ngroups=(n_slc // SUBLANES) * (c + 1) // nvar, tq=tq, n_slc=n_slc, n_sel=n_sel)
    o_cmp = ocmp_sc[...]

    cc = lax.broadcasted_iota(jnp.int32, (tq, width), 0)
    rr = lax.broadcasted_iota(jnp.int32, (tq, width), 1) & (tq - 1)

    bias = bias_sc[...]
    q3_slc = jnp.concatenate([q_top, jnp.concatenate([bias] * NSA_HPG, axis=1)], axis=0)

    def slc_scores(t):
        return _dot(kvs_ref[pl.ds(pl.multiple_of(t * tq, tq), tq), :], q3_slc)

    def slc_values(t):
        return vst_ref[:, pl.ds(pl.multiple_of(t * tq, tq), tq)]

    def win_scores(t):
        return _dot(kvw_ref[pl.ds(pl.multiple_of(t * tq, tq), tq), :], q3)

    def win_values(t):
        return vwt_ref[:, pl.ds(pl.multiple_of(t * tq, tq), tq)]

    back = NSA_WINDOW // tq
    slc = _FlashChain(0, i, slc_scores, slc_values, (sa_sc, sb_sc), acc_sc, width, unroll=8)
    win = _FlashChain(jnp.maximum(i - back, 0), i, win_scores, win_values, (wa_sc, wb_sc), wacc_sc, width)
    slc.start()
    win.start(first_mask=jnp.logical_or(cc > rr, i < back))
    slc.loops()
    win.loops()
    o_slc = slc.finish(cc <= rr)
    o_win = win.finish(cc <= rr)

    gt = gt_ref[:, pl.ds(q0, tq)]
    outs = []
    for h in range(NSA_HPG):
        sl = slice(h * tq, (h + 1) * tq)
        outs.append(gt[3 * h:3 * h + 1] * o_cmp[:, sl] + gt[3 * h + 1:3 * h + 2] * o_slc[:, sl]
                    + gt[3 * h + 2:3 * h + 3] * o_win[:, sl])
    outs.append(jnp.zeros((NSA_DH, tq), F32))
    o_ref[pl.ds(q0, tq), :] = jnp.concatenate(outs, axis=0).T.astype(BF16)


def _nsa_attention(q6t, kvcmp, vct, kvs, vst, kvw, vwt, gate_t, ovt, batch, seq):
    tq = NSA_TILE
    ncp = seq // NSA_CMP_STRIDE
    n_slc = seq // NSA_SLC_LEN
    width = NSA_HPG * tq
    kern = functools.partial(_nsa_kernel, tq=tq, n_slc=n_slc, n_sel=min(NSA_TOPN, n_slc))
    kv_spec = pl.BlockSpec((seq, LANES), lambda b, g: (b, g))
    vt_spec = pl.BlockSpec((V_ROWS, seq), lambda b, g: (g, b))
    return pl.pallas_call(
        kern,
        grid=(batch, NSA_KV_HEADS),
        in_specs=[pl.BlockSpec((NSA_HPG * NSA_DH, seq), lambda b, g: (g, b)),
                  pl.BlockSpec((ncp, LANES), lambda b, g: (b, g)),
                  pl.BlockSpec((V_ROWS, ncp), lambda b, g: (b * NSA_KV_HEADS + g, 0)),
                  kv_spec, vt_spec, kv_spec, vt_spec,
                  pl.BlockSpec((LANES, seq), lambda b, g: (g, b)),
                  _const_spec(ovt.shape)],
        out_specs=pl.BlockSpec((seq, 2 * LANES), lambda b, g: (b, g)),
        out_shape=jax.ShapeDtypeStruct((batch * seq, NSA_KV_HEADS * 2 * LANES), BF16),
        scratch_shapes=[pltpu.VMEM((NSA_DH, tq), BF16), pltpu.VMEM((NSA_DH, width), F32),
                        pltpu.VMEM((V_ROWS, width), F32), pltpu.VMEM((tq, width), F32), pltpu.VMEM((tq, width), F32),
                        pltpu.VMEM((V_ROWS, width), F32), pltpu.VMEM((tq, width), F32), pltpu.VMEM((tq, width), F32)],
        compiler_params=_params(("parallel", "parallel")),
        name="nsa_attention",
    )(q6t, kvcmp, vct, kvs, vst, kvw, vwt, gate_t, ovt)


def _mla_kernel(qt_ref, k_ref, vt_ref, o_ref, acc0_sc, sa0_sc, sb0_sc, acc1_sc, sa1_sc, sb1_sc, *, tq):
    nq = qt_ref.shape[1] // tq
    causal = lax.broadcasted_iota(jnp.int32, (tq, tq), 0) <= lax.broadcasted_iota(jnp.int32, (tq, tq), 1)

    def q_tile(i, carry):
        q0 = pl.multiple_of(i * tq, tq)
        chains = []
        for hh, acc_sc, bufs in ((0, acc0_sc, (sa0_sc, sb0_sc)), (1, acc1_sc, (sa1_sc, sb1_sc))):
            def scores(t, hh=hh):
                off = pl.multiple_of(t * tq, tq)
                return _dot(k_ref[pl.ds(off, tq), hh * LANES:(hh + 1) * LANES],
                            qt_ref[hh * LANES:(hh + 1) * LANES, pl.ds(q0, tq)])

            def values(t, hh=hh):
                return vt_ref[hh * V_ROWS:(hh + 1) * V_ROWS, pl.ds(pl.multiple_of(t * tq, tq), tq)]

            chains.append(_FlashChain(0, i, scores, values, bufs, acc_sc, tq, unroll=4))
        for chain in chains:
            chain.start()
        for chain in chains:
            chain.loops()
        outs = [chain.finish(causal) for chain in chains]
        o_ref[pl.ds(q0, tq), :] = jnp.concatenate(outs, axis=0).T.astype(BF16)
        return carry

    lax.fori_loop(0, nq, q_tile, 0)


def _mla_attention(mqt, mk, mvt, batch, seq):
    tq = MLA_TILE
    pairs = MLA_HEADS // 2
    return pl.pallas_call(
        functools.partial(_mla_kernel, tq=tq),
        grid=(batch, pairs),
        in_specs=[pl.BlockSpec((2 * LANES, seq), lambda b, p: (p, b)),
                  pl.BlockSpec((seq, 2 * LANES), lambda b, p: (b, p)),
                  pl.BlockSpec((2 * V_ROWS, seq), lambda b, p: (p, b))],
        out_specs=pl.BlockSpec((seq, LANES), lambda b, p: (b, p)),
        out_shape=jax.ShapeDtypeStruct((batch * seq, pairs * LANES), BF16),
        scratch_shapes=[pltpu.VMEM((V_ROWS, tq), F32), pltpu.VMEM((tq, tq), F32), pltpu.VMEM((tq, tq), F32)] * 2,
        compiler_params=_params(("parallel", "parallel")),
        name="mla_attention",
    )(mqt, mk, mvt)


def _out_ffn_kernel(x_ref, on_ref, om_ref, u_ref, uh_ref, pw_ref, ps_ref, won_ref, wom_ref, wop_ref, g2_ref,
                    w1_ref, w2_ref, gf_ref, y_ref, *, tm, seq, final):
    i = pl.program_id(0)
    t0 = (i * tm) % seq
    u = u_ref[...]
    halo = jnp.where(t0 == 0, 0.0, uh_ref[...])
    ext = jnp.concatenate([halo, u], axis=0)
    s2 = ext + pltpu.roll(ext, 1, axis=0)
    s4 = s2 + pltpu.roll(s2, 2, axis=0)
    s8 = s4 + pltpu.roll(s4, 4, axis=0)
    s16 = s8 + pltpu.roll(s8, 8, axis=0)
    sl = slice(POOL_HALO, POOL_HALO + tm)
    lane = lax.broadcasted_iota(jnp.int32, (tm, POOL_WIDTH), 1)
    t = t0 + lax.broadcasted_iota(jnp.int32, (tm, POOL_WIDTH), 0)
    sums = jnp.where(lane < 64, s2[sl], jnp.where(lane < 128, s4[sl], jnp.where(lane < 192, s8[sl], s16[sl])))
    win = jnp.where(lane < 64, 2, jnp.where(lane < 128, 4, jnp.where(lane < 192, 8, 16)))
    cnt = jnp.minimum(t + 1, win).astype(F32)
    pooled = (sums / cnt - u).astype(BF16)
    y_pool = (_dot(pooled, pw_ref[...]) * ps_ref[...]).astype(BF16)
    mix = _dot(on_ref[...], won_ref[...]) + _dot(om_ref[...], wom_ref[...]) + _dot(y_pool, wop_ref[...])
    x = x_ref[...] + mix
    h = _rms(x, g2_ref[...]).astype(BF16)
    ff = None
    for c in range(D_FF // FF_CHUNK):
        cs = slice(c * FF_CHUNK, (c + 1) * FF_CHUNK)
        a = jnp.maximum(_dot(h, w1_ref[:, cs]), 0.0)
        part = _dot((a * a).astype(BF16), w2_ref[cs, :])
        ff = part if ff is None else ff + part
    acc = x + ff
    if final:
        acc = _rms(acc, gf_ref[...])
    y_ref[...] = acc


def _out_ffn(x, o_nsa, o_mla, u, pw, ps, won, wom, wop, g2, w1, w2, gf, seq, final):
    n = x.shape[0]
    tm = ROW_TILE
    hb = tm // POOL_HALO
    row = lambda width: pl.BlockSpec((tm, width), lambda i: (i, 0))
    return pl.pallas_call(
        functools.partial(_out_ffn_kernel, tm=tm, seq=seq, final=final),
        grid=(n // tm,),
        in_specs=[row(D_MODEL), row(o_nsa.shape[1]), row(o_mla.shape[1]), row(POOL_WIDTH),
                  pl.BlockSpec((POOL_HALO, POOL_WIDTH), lambda i: (jnp.maximum(i * hb - 1, 0), 0)),
                  _const_spec(pw.shape), _const_spec(ps.shape), _const_spec(won.shape), _const_spec(wom.shape),
                  _const_spec(wop.shape), _const_spec(g2.shape), _const_spec(w1.shape), _const_spec(w2.shape),
                  _const_spec(gf.shape)],
        out_specs=row(D_MODEL),
        out_shape=jax.ShapeDtypeStruct((n, D_MODEL), F32),
        compiler_params=_params(("parallel",)),
        name="out_ffn",
    )(x, o_nsa, o_mla, u, u, pw, ps, won, wom, wop, g2, w1, w2, gf)


def _overlap_t(seq):
    ncp = seq // NSA_CMP_STRIDE
    n_slc = seq // NSA_SLC_LEN
    start = np.arange(ncp - 1)[None, :] * NSA_CMP_STRIDE
    end = start + NSA_CMP_LEN
    s0 = np.arange(n_slc)[:, None] * NSA_SLC_LEN
    s1 = s0 + NSA_SLC_LEN
    ov = np.clip(np.minimum(end, s1) - np.maximum(start, s0), 0, None) / NSA_CMP_LEN
    out = np.zeros((LANES, ncp), np.float32)
    out[:n_slc, :ncp - 1] = ov
    return jnp.asarray(out, BF16)


def kernel(x, positions, ln1_g, w_in, nsa_cmp_pos, nsa_cmp_w1, nsa_cmp_w2, mla_q_norm, mla_w_qup, mla_kv_norm,
           mla_w_kvup, pool_w, pool_scale, w_out, ln2_g, w_ff1, w_ff2, final_g):
    batch, seq, _ = x.shape
    depth = w_in.shape[0]
    n = batch * seq
    assert n % min(ROPE_TILE, n) == 0 and n % ROW_TILE == 0
    assert seq % ROW_TILE == 0 and seq % NSA_TILE == 0 and seq % MLA_TILE == 0
    assert seq // NSA_SLC_LEN <= NSA_DH and NSA_WINDOW % NSA_TILE == 0 and NSA_TILE % NSA_SLC_LEN == 0

    cos_t, sin_t = _rope_tables(positions)
    ovt = _overlap_t(seq)
    in_layout, in_layout_t = _in_layout(), _in_layout_t()
    lq, lqr, lk, lv = _mla_layouts()
    won_layout = _layout([r for g in range(NSA_KV_HEADS)
                          for r in ((g * NSA_HPG * NSA_DH, NSA_HPG * NSA_DH, 1.0), (None, NSA_DH, 0.0))])
    row2 = lambda v: v.reshape(1, -1)

    xf = x.reshape(n, D_MODEL)
    for l in range(depth):
        w_all = _take_cols(w_in[l], in_layout)
        wt_all = _take_cols(w_in[l], in_layout_t).T
        wqt, wqrt = _take_cols(mla_w_qup[l], lq).T, _take_cols(mla_w_qup[l], lqr).T
        wk, wvt = _take_cols(mla_w_kvup[l], lk), _take_cols(mla_w_kvup[l], lv).T
        q6t, vst, vwt, gate_t, kvc, kvs, kvw, mqt, mk, mvt, u = _proj_in(
            xf, row2(ln1_g[l]), w_all, wt_all, cos_t, sin_t, row2(mla_q_norm[l]), row2(mla_kv_norm[l]),
            wqt, wqrt, wk, wvt, seq)

        w1c = _blockdiag2(nsa_cmp_w1[l, 0], nsa_cmp_w1[l, 1]).astype(BF16)
        w2c = _blockdiag2(nsa_cmp_w2[l, 0], nsa_cmp_w2[l, 1]).astype(BF16)
        pos = jnp.concatenate([nsa_cmp_pos[l, 0], nsa_cmp_pos[l, 1]], axis=-1)[:, None, :]
        half = NSA_CMP_STRIDE
        kvcmp, vct = _compress(kvc, pos[:half], pos[half:], w1c[:half], w1c[half:], w2c, batch, seq)

        o_nsa = _nsa_attention(q6t, kvcmp, vct, kvs, vst, kvw, vwt, gate_t, ovt, batch, seq)
        o_mla = _mla_attention(mqt, mk, mvt, batch, seq)

        pw = _blockdiag2(_blockdiag2(pool_w[l, 0], pool_w[l, 1]), _blockdiag2(pool_w[l, 2], pool_w[l, 3])).astype(BF16)
        won = _take_rows(w_out[l], won_layout)
        wom = w_out[l, D_MIX_NSA:D_MIX_NSA + D_MIX_MLA].astype(BF16)
        wop = w_out[l, D_MIX_NSA + D_MIX_MLA:].astype(BF16)
        xf = _out_ffn(xf, o_nsa, o_mla, u, pw, row2(pool_scale[l]), won, wom, wop, row2(ln2_g[l]),
                      w_ff1[l].astype(BF16), w_ff2[l].astype(BF16), row2(final_g), seq, final=(l == depth - 1))
    return xf.reshape(batch, seq, D_MODEL)
```

```python
import functools

import numpy as np
import jax
import jax.numpy as jnp
from jax import lax
from jax.experimental import pallas as pl
from jax.experimental.pallas import tpu as pltpu

F32 = jnp.float32
BF16 = jnp.bfloat16

D_MODEL = 1024
NSA_HEADS = 6
NSA_KV_HEADS = 2
NSA_HPG = 3
NSA_DH = 64
NSA_CMP_LEN = 32
NSA_CMP_STRIDE = 16
NSA_SLC_LEN = 64
NSA_TOPN = 16
NSA_WINDOW = 512
NSA_FORCE = 1.0e4
MLA_HEADS = 6
MLA_NOPE = 64
MLA_ROPE = 32
MLA_VDIM = 64
V_ROWS = 80
ROPE_THETA = 10000.0
POOL_WIDTH = 256
POOL_HALO = 16
D_FF = 4096
EPS = 1e-6

COL_NSA_KV = NSA_HEADS * NSA_DH
COL_GATE = COL_NSA_KV + 3 * 2 * NSA_KV_HEADS * NSA_DH
COL_CQ = COL_GATE + 3 * NSA_HEADS
MLA_Q_LORA = 256
MLA_KV_LORA = 128
COL_CKV = COL_CQ + MLA_Q_LORA
COL_KR = COL_CKV + MLA_KV_LORA
COL_POOL = COL_KR + MLA_ROPE
D_MIX_NSA = NSA_HEADS * NSA_DH
D_MIX_MLA = MLA_HEADS * MLA_VDIM

LANES = 128
SUBLANES = 8
VMEM_LIMIT = 56 * 1024 * 1024

MASKED = -1.0e30
M_FLOOR = -1.0e20
LOG2E = 1.4426950408889634

ROW_TILE = 512
ROPE_TILE = 4096
NSA_TILE = 256
MLA_TILE = 512
FF_CHUNK = 1024
SELECT_VARIANTS = 4


def _dot(a, b):
    return jnp.dot(a, b, preferred_element_type=F32)


def _dot_nt(a, b):
    return lax.dot_general(a, b, (((1,), (1,)), ((), ())), preferred_element_type=F32)


def _rms(x, g):
    return x * lax.rsqrt(jnp.mean(x * x, axis=-1, keepdims=True) + EPS) * g


def _layout(spec):
    return tuple(spec)


def _nkv_col(br, kvi, g):
    return COL_NSA_KV + (br * 2 + kvi) * NSA_KV_HEADS * NSA_DH + g * NSA_DH


def _in_layout():
    spec = []
    for br in range(3):
        for g in range(2):
            spec += [(_nkv_col(br, 0, g), 64, 1.0)]
            spec += [(None, 64, 0.0)] if br == 1 else [(_nkv_col(br, 1, g), 64, 1.0)]
    spec += [(COL_CQ, MLA_Q_LORA, 1.0), (COL_CKV, MLA_KV_LORA, 1.0)]
    spec += [(COL_POOL, POOL_WIDTH, 1.0)]
    return _layout(spec)


def _in_layout_t():
    spec = []
    spec += [(0, NSA_HEADS * NSA_DH, 0.125)]
    for br in (1, 2):
        for g in range(2):
            spec += [(_nkv_col(br, 1, g), 64, 1.0), (None, V_ROWS - 64, 0.0)]
    for g in range(2):
        spec += [(COL_GATE + g * 3 * NSA_HPG, 3 * NSA_HPG, 1.0), (None, LANES - 3 * NSA_HPG, 0.0)]
    half = MLA_ROPE // 2
    spec += [(None, MLA_NOPE, 0.0), (COL_KR, MLA_ROPE, 1.0), (None, LANES - MLA_NOPE - MLA_ROPE, 0.0)]
    spec += [(None, MLA_NOPE, 0.0), (COL_KR + half, half, -1.0), (COL_KR, half, 1.0),
             (None, LANES - MLA_NOPE - MLA_ROPE, 0.0)]
    return _layout(spec)


def _mla_layouts():
    q, qr, k, v = [], [], [], []
    for h in range(MLA_HEADS):
        b = h * 96
        q += [(b, 96, 1.0), (None, 32, 0.0)]
        qr += [(None, 64, 0.0), (b + 80, 16, -1.0), (b + 64, 16, 1.0), (None, 32, 0.0)]
        k += [(h * 128, 64, 1.0), (None, 64, 0.0)]
        v += [(h * 128 + 64, 64, 1.0), (None, V_ROWS - 64, 0.0)]
    return _layout(q), _layout(qr), _layout(k), _layout(v)


def _take_rows(w, layout):
    cols = w.shape[1]
    pieces = [jnp.zeros((n, cols), w.dtype) if start is None else
              (w[start:start + n] if scale == 1.0 else w[start:start + n] * scale)
              for start, n, scale in layout]
    return jnp.concatenate(pieces, axis=0).astype(BF16)


def _blockdiag2(a, b):
    z = jnp.zeros_like(a)
    top = jnp.concatenate([a, z], axis=-1)
    bot = jnp.concatenate([z, b], axis=-1)
    return jnp.concatenate([top, bot], axis=-2)


def _const_spec(shape):
    nd = len(shape)
    return pl.BlockSpec(shape, lambda *_: (0,) * nd, pipeline_mode=pl.Buffered(1))


def _params(sem):
    return pltpu.CompilerParams(dimension_semantics=sem, vmem_limit_bytes=VMEM_LIMIT)


def _rope_kernel(pos_ref, inv_ref, cos_ref, sin_ref):
    ang = inv_ref[...] * pos_ref[...].astype(F32)
    cos_ref[...] = jnp.cos(ang)
    sin_ref[...] = jnp.sin(ang)


def _rope_tables(positions):
    n = positions.size
    half = MLA_ROPE // 2
    inv = jnp.power(jnp.float32(ROPE_THETA), -(jnp.arange(half, dtype=F32) / half))
    tm = min(ROPE_TILE, n)
    out = jax.ShapeDtypeStruct((half, n), F32)
    return pl.pallas_call(
        _rope_kernel,
        grid=(n // tm,),
        in_specs=[pl.BlockSpec((1, tm), lambda i: (0, i)), _const_spec((half, 1))],
        out_specs=[pl.BlockSpec((half, tm), lambda i: (0, i))] * 2,
        out_shape=[out, out],
        compiler_params=_params(("parallel",)),
        name="rope_tables",
    )(positions.reshape(1, n), inv.reshape(half, 1))


def _ones_rows(rows, cols):
    r = lax.broadcasted_iota(jnp.int32, (rows, cols), 0) % V_ROWS
    return jnp.where(r >= 64, 1.0, 0.0)


def _proj_in_kernel(x_ref, g_ref, w_ref, wt_ref, cost_ref, sint_ref, qg_ref, kvg_ref,
                    wqt_ref, wqrt_ref, wk_ref, wvt_ref,
                    q6t_ref, vst_ref, vwt_ref, gatet_ref, kvc_ref, kvs_ref, kvw_ref, mqt_ref, mk_ref, mvt_ref, u_ref,
                    *, seq):
    h = _rms(x_ref[...], g_ref[...]).astype(BF16)
    zt = _dot_nt(wt_ref[...], h)
    tm = h.shape[0]
    nv = NSA_KV_HEADS * V_ROWS
    nq = NSA_HEADS * NSA_DH
    q6t_ref[...] = (zt[0:nq] * LOG2E).astype(BF16)
    vst_ref[...] = (zt[nq:nq + nv] + _ones_rows(nv, tm)).astype(BF16)
    vwt_ref[...] = (zt[nq + nv:nq + 2 * nv] + _ones_rows(nv, tm)).astype(BF16)
    r0 = nq + 2 * nv
    gatet_ref[...] = jax.nn.sigmoid(zt[r0:r0 + NSA_KV_HEADS * LANES])
    kr_t = zt[r0 + NSA_KV_HEADS * LANES:r0 + (NSA_KV_HEADS + 1) * LANES]
    krrot_t = zt[r0 + (NSA_KV_HEADS + 1) * LANES:r0 + (NSA_KV_HEADS + 2) * LANES]
    kvw = NSA_KV_HEADS * LANES
    z = _dot_nt(h, w_ref[0:3 * kvw, :])
    kvc_ref[...] = z[:, 0:kvw]
    tok = (pl.program_id(0) * tm) % seq + lax.broadcasted_iota(jnp.int32, (tm, 2 * LANES), 0)
    lane = lax.broadcasted_iota(jnp.int32, (tm, 2 * LANES), 1) & (LANES - 1)
    onehot = jnp.where(lane - NSA_DH == tok // NSA_SLC_LEN, 1.0, 0.0)
    kvs_ref[...] = (z[:, kvw:2 * kvw] + onehot).astype(BF16)
    kvw_ref[...] = z[:, 2 * kvw:3 * kvw].astype(BF16)
    lat = MLA_Q_LORA + MLA_KV_LORA
    z = _dot_nt(h, w_ref[3 * kvw:3 * kvw + lat + POOL_WIDTH, :])
    u_ref[...] = z[:, lat:lat + POOL_WIDTH]
    cqn = _rms(z[:, 0:MLA_Q_LORA], qg_ref[...]).astype(BF16)
    qa = _dot_nt(wqt_ref[...], cqn)
    qb = _dot_nt(wqrt_ref[...], cqn)
    c16, s16 = cost_ref[...], sint_ref[...]
    cos_t = jnp.concatenate([jnp.ones((64, tm), F32), c16, c16, jnp.ones((32, tm), F32)], axis=0)
    sin_t = jnp.concatenate([jnp.zeros((64, tm), F32), s16, s16, jnp.zeros((32, tm), F32)], axis=0)
    scale = (MLA_NOPE + MLA_ROPE) ** -0.5 * LOG2E
    for hh in range(MLA_HEADS):
        sl = slice(hh * LANES, (hh + 1) * LANES)
        mqt_ref[sl, :] = ((qa[sl] * cos_t + qb[sl] * sin_t) * scale).astype(BF16)
    ckvn = _rms(z[:, MLA_Q_LORA:lat], kvg_ref[...]).astype(BF16)
    k_rope = (kr_t * cos_t + krrot_t * sin_t).T
    ka = _dot_nt(ckvn, wk_ref[...])
    for hh in range(MLA_HEADS):
        sl = slice(hh * LANES, (hh + 1) * LANES)
        mk_ref[:, sl] = (ka[:, sl] + k_rope).astype(BF16)
    mvt_ref[...] = (_dot_nt(wvt_ref[...], ckvn) + _ones_rows(MLA_HEADS * V_ROWS, tm)).astype(BF16)


def _proj_in(x, g, w_all, wt_all, cos_t, sin_t, qg, kvg, wqt, wqrt, wk, wvt, seq):
    n = x.shape[0]
    tm = ROW_TILE
    row = lambda width: pl.BlockSpec((tm, width), lambda i: (i, 0))
    col = lambda height: pl.BlockSpec((height, tm), lambda i: (0, i))
    nv = NSA_KV_HEADS * V_ROWS
    outs = [("t", NSA_HEADS * NSA_DH, BF16), ("t", nv, BF16), ("t", nv, BF16), ("t", 256, F32), ("r", 256, F32),
            ("r", 256, BF16), ("r", 256, BF16), ("t", 768, BF16), ("r", 768, BF16), ("t", MLA_HEADS * V_ROWS, BF16),
            ("r", 256, F32)]
    return pl.pallas_call(
        functools.partial(_proj_in_kernel, seq=seq),
        grid=(n // tm,),
        in_specs=[row(D_MODEL), _const_spec(g.shape), _const_spec(w_all.shape), _const_spec(wt_all.shape),
                  col(MLA_ROPE // 2), col(MLA_ROPE // 2), _const_spec(qg.shape), _const_spec(kvg.shape),
                  _const_spec(wqt.shape), _const_spec(wqrt.shape), _const_spec(wk.shape), _const_spec(wvt.shape)],
        out_specs=[col(w) if kind == "t" else row(w) for kind, w, _ in outs],
        out_shape=[jax.ShapeDtypeStruct((w, n) if kind == "t" else (n, w), dt) for kind, w, dt in outs],
        compiler_params=_params(("parallel",)),
        name="proj_in",
    )(x, g, w_all, wt_all, cos_t, sin_t, qg, kvg, wqt, wqrt, wk, wvt)


def _compress_kernel(kvc_ref, posa_ref, posb_ref, wa_ref, wb_ref, w2_ref, o_ref, vt_ref, *, ncp):
    a = jnp.zeros((ncp, LANES), F32)
    b = jnp.zeros((ncp, LANES), F32)
    for l in range(NSA_CMP_STRIDE):
        xl = kvc_ref[pl.ds(l, ncp, stride=NSA_CMP_STRIDE), :]
        a = a + _dot((xl + posa_ref[l]).astype(BF16), wa_ref[l])
        b = b + _dot((xl + posb_ref[l]).astype(BF16), wb_ref[l])
    pre = a + pltpu.roll(b, ncp - 1, axis=0)
    hid = jax.nn.gelu(pre).astype(BF16)
    out = _dot(hid, w2_ref[...])
    o_ref[...] = out.astype(BF16)
    ones = jnp.ones((V_ROWS - NSA_DH, ncp), F32)
    vt_ref[...] = jnp.concatenate([out.T[NSA_DH:2 * NSA_DH], ones], axis=0).astype(BF16)


def _compress(kvc, posa, posb, wa, wb, w2, batch, seq):
    ncp = seq // NSA_CMP_STRIDE
    return pl.pallas_call(
        functools.partial(_compress_kernel, ncp=ncp),
        grid=(batch, NSA_KV_HEADS),
        in_specs=[pl.BlockSpec((seq, LANES), lambda b, g: (b, g)), _const_spec(posa.shape), _const_spec(posb.shape),
                  _const_spec(wa.shape), _const_spec(wb.shape), _const_spec(w2.shape)],
        out_specs=[pl.BlockSpec((ncp, LANES), lambda b, g: (b, g)),
                   pl.BlockSpec((V_ROWS, ncp), lambda b, g: (b * NSA_KV_HEADS + g, 0))],
        out_shape=[jax.ShapeDtypeStruct((batch * ncp, 2 * LANES), BF16),
                   jax.ShapeDtypeStruct((batch * NSA_KV_HEADS * V_ROWS, ncp), BF16)],
        compiler_params=_params(("parallel", "parallel")),
        name="nsa_compress",
    )(kvc, posa, posb, wa, wb, w2)


def _flash_update(buf, v_t, m, acc_ref, mask=None):
    load = (lambda: buf[...]) if mask is None else (lambda: jnp.where(mask, buf[...], MASKED))
    m_new = jnp.maximum(m, jnp.max(load(), axis=0, keepdims=True))
    alpha = jnp.exp2(m - m_new)
    p = jnp.exp2(load() - m_new)
    acc_ref[...] = alpha * acc_ref[...] + _dot(v_t, p.astype(BF16))
    return m_new


class _FlashChain:
    def __init__(self, first, last, scores, values, bufs, acc_ref, width, unroll=2):
        self.first, self.last, self.scores, self.values = first, last, scores, values
        self.bufs, self.acc_ref, self.width, self.unroll = bufs, acc_ref, width, unroll
        self.m = None

    def start(self, first_mask=None):
        s_first = self.scores(self.first)
        self.bufs[0][...] = s_first if first_mask is None else jnp.where(first_mask, s_first, MASKED)
        self.acc_ref[...] = jnp.zeros(self.acc_ref.shape, F32)
        self.m = jnp.full((1, self.width), M_FLOOR, F32)

    def loops(self):
        bufs, acc_ref, scores, values = self.bufs, self.acc_ref, self.scores, self.values

        def steps(count, t0):
            def body(jj, m):
                for k in range(count):
                    t = t0 + count * jj + k
                    cur = bufs[k % 2]
                    if count > 1:
                        bufs[(k + 1) % 2][...] = scores(t + 1)
                    m = _flash_update(cur, values(t), m, acc_ref)
                    if count == 1:
                        cur[...] = scores(t + 1)
                return m
            return body

        t0, rem, level = self.first, self.last - self.first, self.unroll
        while level >= 1:
            trips = rem // level
            self.m = lax.fori_loop(0, trips, steps(level, t0), self.m)
            t0, rem, level = t0 + trips * level, rem - trips * level, level // 2

    def finish(self, last_mask):
        _flash_update(self.bufs[0], self.values(self.last), self.m, self.acc_ref, mask=last_mask)
        return self.acc_ref[0:64] * (1.0 / self.acc_ref[64:65])


def _nsa_select(q3, q0, kvc_ref, vct_ref, ovt_ref, ocmp_ref, bias_ref, *, rows, ngroups, tq, n_slc, n_sel):
    width = NSA_HPG * tq
    ncp = kvc_ref.shape[0]
    s = _dot(kvc_ref[0:rows, :], q3)
    nblk = lax.broadcasted_iota(jnp.int32, (rows, width), 0)
    qpos = q0 + (lax.broadcasted_iota(jnp.int32, (1, width), 1) & (tq - 1))
    last_blk = (qpos - (NSA_CMP_LEN - 1)) >> 4
    s = jnp.where(nblk <= last_blk, s, -jnp.inf)
    m = jnp.max(s, axis=0, keepdims=True)
    m = jnp.where(m == -jnp.inf, 0.0, m)
    e = jnp.exp2(s - m)
    d = jnp.sum(e, axis=0, keepdims=True)
    p = e * (1.0 / jnp.where(d > 0.0, d, 1.0))

    def pad(a):
        return a if rows == ncp else jnp.concatenate([a, jnp.zeros((ncp - rows, a.shape[1]), a.dtype)], axis=0)

    ocmp_ref[...] = _dot(vct_ref[0:NSA_DH, :], pad(p.astype(BF16)))

    every_valid_block_selected = q0 + tq <= n_sel * NSA_SLC_LEN

    @pl.when(every_valid_block_selected)
    def _():
        bias_ref[...] = jnp.zeros(bias_ref.shape, BF16)

    @pl.when(jnp.logical_not(every_valid_block_selected))
    def _():
        psum = p[:, 0:tq] + p[:, tq:2 * tq] + p[:, 2 * tq:3 * tq]
        hi = psum.astype(BF16)
        lo = (psum - hi.astype(F32)).astype(BF16)
        ovt = ovt_ref[...]
        imp = _dot(ovt, pad(hi)) + _dot(ovt, pad(lo))
        nblocks = ngroups * SUBLANES
        blk = lax.broadcasted_iota(jnp.int32, (LANES, tq), 0)
        qp = q0 + lax.broadcasted_iota(jnp.int32, (LANES, tq), 1)
        cur = qp >> 6
        forced = (blk == 0) | (blk == cur) | (blk == cur - 1)
        valid = (blk << 6) <= qp
        score = jnp.where(forced, NSA_FORCE, jnp.where(valid, imp, -1.0))[0:nblocks]
        sub = SUBLANES
        sidx = lax.broadcasted_iota(jnp.int32, (sub, tq), 0)
        groups = [score[k * sub:(k + 1) * sub] for k in range(ngroups)]
        cnts = [jnp.zeros((sub, tq), F32) for _ in groups]
        for sp in range(nblocks):
            col = score[sp:sp + 1, :]
            for k, grp in enumerate(groups):
                if k * sub > sp:
                    cnts[k] = cnts[k] + jnp.where(col >= grp, 1.0, 0.0)
                elif (k + 1) * sub - 1 <= sp:
                    cnts[k] = cnts[k] + jnp.where(col > grp, 1.0, 0.0)
                else:
                    after = jnp.where(col >= grp, 1.0, 0.0)
                    cnts[k] = cnts[k] + jnp.where(sidx + k * sub > sp, after, jnp.where(col > grp, 1.0, 0.0))
        bias = jnp.where(jnp.concatenate(cnts, axis=0) < n_sel, 0.0, MASKED)
        if nblocks < NSA_DH:
            bias = jnp.concatenate([bias, jnp.zeros((NSA_DH - nblocks, tq), F32)], axis=0)
        bias_ref[...] = bias.astype(BF16)


def _nsa_kernel(qt_ref, kvc_ref, vct_ref, kvs_ref, vst_ref, kvw_ref, vwt_ref, gt_ref, ovt_ref, o_ref,
                bias_sc, ocmp_sc, acc_sc, sa_sc, sb_sc, wacc_sc, wa_sc, wb_sc, *, tq, n_slc, n_sel):
    nq = qt_ref.shape[1] // tq

    def q_tile(i, carry):
        _nsa_q_tile(i, qt_ref, kvc_ref, vct_ref, kvs_ref, vst_ref, kvw_ref, vwt_ref, gt_ref, ovt_ref, o_ref,
                    bias_sc, ocmp_sc, acc_sc, sa_sc, sb_sc, wacc_sc, wa_sc, wb_sc, tq=tq, n_slc=n_slc, n_sel=n_sel)
        return carry

    lax.fori_loop(0, nq, q_tile, 0)


def _nsa_q_tile(i, qt_ref, kvc_ref, vct_ref, kvs_ref, vst_ref, kvw_ref, vwt_ref, gt_ref, ovt_ref, o_ref,
                bias_sc, ocmp_sc, acc_sc, sa_sc, sb_sc, wacc_sc, wa_sc, wb_sc, *, tq, n_slc, n_sel):
    q0 = pl.multiple_of(i * tq, tq)
    width = NSA_HPG * tq
    q_top = jnp.concatenate([qt_ref[h * NSA_DH:(h + 1) * NSA_DH, pl.ds(q0, tq)] for h in range(NSA_HPG)],
                            axis=1)
    q3 = jnp.concatenate([q_top, jnp.zeros((NSA_DH, width), BF16)], axis=0)

    ncp = kvc_ref.shape[0]
    nq = ncp * NSA_CMP_STRIDE // tq
    divisible = all(v % SELECT_VARIANTS == 0 for v in (nq, ncp // 16, n_slc // SUBLANES))
    nvar = SELECT_VARIANTS if divisible else 1
    for c in range(nvar):
        @pl.when((i * nvar) // nq == c)
        def _(c=c):
            _nsa_select(q3, q0, kvc_ref, vct_ref, ovt_ref, ocmp_sc, bias_sc, rows=ncp * (c + 1) // nvar,
                        ngroups=(n_slc // SUBLANES) * (c + 1) // nvar, tq=tq, n_slc=n_slc, n_sel=n_sel)
    o_cmp = ocmp_sc[...]

    cc = lax.broadcasted_iota(jnp.int32, (tq, width), 0)
    rr = lax.broadcasted_iota(jnp.int32, (tq, width), 1) & (tq - 1)

    bias = bias_sc[...]
    q3_slc = jnp.concatenate([q_top, jnp.concatenate([bias] * NSA_HPG, axis=1)], axis=0)

    def slc_scores(t):
        return _dot(kvs_ref[pl.ds(pl.multiple_of(t * tq, tq), tq), :], q3_slc)

    def slc_values(t):
        return vst_ref[:, pl.ds(pl.multiple_of(t * tq, tq), tq)]

    def win_scores(t):
        return _dot(kvw_ref[pl.ds(pl.multiple_of(t * tq, tq), tq), :], q3)

    def win_values(t):
        return vwt_ref[:, pl.ds(pl.multiple_of(t * tq, tq), tq)]

    back = NSA_WINDOW // tq
    slc = _FlashChain(0, i, slc_scores, slc_values, (sa_sc, sb_sc), acc_sc, width, unroll=8)
    win = _FlashChain(jnp.maximum(i - back, 0), i, win_scores, win_values, (wa_sc, wb_sc), wacc_sc, width)
    slc.start()
    win.start(first_mask=jnp.logical_or(cc > rr, i < back))
    slc.loops()
    win.loops()
    o_slc = slc.finish(cc <= rr)
    o_win = win.finish(cc <= rr)

    gt = gt_ref[:, pl.ds(q0, tq)]
    outs = []
    for h in range(NSA_HPG):
        sl = slice(h * tq, (h + 1) * tq)
        outs.append(gt[3 * h:3 * h + 1] * o_cmp[:, sl] + gt[3 * h + 1:3 * h + 2] * o_slc[:, sl]
                    + gt[3 * h + 2:3 * h + 3] * o_win[:, sl])
    outs.append(jnp.zeros((NSA_DH, tq), F32))
    o_ref[pl.ds(q0, tq), :] = jnp.concatenate(outs, axis=0).T.astype(BF16)


def _nsa_attention(q6t, kvcmp, vct, kvs, vst, kvw, vwt, gate_t, ovt, batch, seq):
    tq = NSA_TILE
    ncp = seq // NSA_CMP_STRIDE
    n_slc = seq // NSA_SLC_LEN
    width = NSA_HPG * tq
    kern = functools.partial(_nsa_kernel, tq=tq, n_slc=n_slc, n_sel=min(NSA_TOPN, n_slc))
    kv_spec = pl.BlockSpec((seq, LANES), lambda b, g: (b, g))
    vt_spec = pl.BlockSpec((V_ROWS, seq), lambda b, g: (g, b))
    return pl.pallas_call(
        kern,
        grid=(batch, NSA_KV_HEADS),
        in_specs=[pl.BlockSpec((NSA_HPG * NSA_DH, seq), lambda b, g: (g, b)),
                  pl.BlockSpec((ncp, LANES), lambda b, g: (b, g)),
                  pl.BlockSpec((V_ROWS, ncp), lambda b, g: (b * NSA_KV_HEADS + g, 0)),
                  kv_spec, vt_spec, kv_spec, vt_spec,
                  pl.BlockSpec((LANES, seq), lambda b, g: (g, b)),
                  _const_spec(ovt.shape)],
        out_specs=pl.BlockSpec((seq, 2 * LANES), lambda b, g: (b, g)),
        out_shape=jax.ShapeDtypeStruct((batch * seq, NSA_KV_HEADS * 2 * LANES), BF16),
        scratch_shapes=[pltpu.VMEM((NSA_DH, tq), BF16), pltpu.VMEM((NSA_DH, width), F32),
                        pltpu.VMEM((V_ROWS, width), F32), pltpu.VMEM((tq, width), F32), pltpu.VMEM((tq, width), F32),
                        pltpu.VMEM((V_ROWS, width), F32), pltpu.VMEM((tq, width), F32), pltpu.VMEM((tq, width), F32)],
        compiler_params=_params(("parallel", "parallel")),
        name="nsa_attention",
    )(q6t, kvcmp, vct, kvs, vst, kvw, vwt, gate_t, ovt)


def _mla_kernel(qt_ref, k_ref, vt_ref, o_ref, acc0_sc, sa0_sc, sb0_sc, acc1_sc, sa1_sc, sb1_sc, *, tq):
    nq = qt_ref.shape[1] // tq
    causal = lax.broadcasted_iota(jnp.int32, (tq, tq), 0) <= lax.broadcasted_iota(jnp.int32, (tq, tq), 1)

    def q_tile(i, carry):
        q0 = pl.multiple_of(i * tq, tq)
        chains = []
        for hh, acc_sc, bufs in ((0, acc0_sc, (sa0_sc, sb0_sc)), (1, acc1_sc, (sa1_sc, sb1_sc))):
            def scores(t, hh=hh):
                off = pl.multiple_of(t * tq, tq)
                return _dot(k_ref[pl.ds(off, tq), hh * LANES:(hh + 1) * LANES],
                            qt_ref[hh * LANES:(hh + 1) * LANES, pl.ds(q0, tq)])

            def values(t, hh=hh):
                return vt_ref[hh * V_ROWS:(hh + 1) * V_ROWS, pl.ds(pl.multiple_of(t * tq, tq), tq)]

            chains.append(_FlashChain(0, i, scores, values, bufs, acc_sc, tq, unroll=4))
        for chain in chains:
            chain.start()
        for chain in chains:
            chain.loops()
        outs = [chain.finish(causal) for chain in chains]
        o_ref[pl.ds(q0, tq), :] = jnp.concatenate(outs, axis=0).T.astype(BF16)
        return carry

    lax.fori_loop(0, nq, q_tile, 0)


def _mla_attention(mqt, mk, mvt, batch, seq):
    tq = MLA_TILE
    pairs = MLA_HEADS // 2
    return pl.pallas_call(
        functools.partial(_mla_kernel, tq=tq),
        grid=(batch, pairs),
        in_specs=[pl.BlockSpec((2 * LANES, seq), lambda b, p: (p, b)),
                  pl.BlockSpec((seq, 2 * LANES), lambda b, p: (b, p)),
                  pl.BlockSpec((2 * V_ROWS, seq), lambda b, p: (p, b))],
        out_specs=pl.BlockSpec((seq, LANES), lambda b, p: (b, p)),
        out_shape=jax.ShapeDtypeStruct((batch * seq, pairs * LANES), BF16),
        scratch_shapes=[pltpu.VMEM((V_ROWS, tq), F32), pltpu.VMEM((tq, tq), F32), pltpu.VMEM((tq, tq), F32)] * 2,
        compiler_params=_params(("parallel", "parallel")),
        name="mla_attention",
    )(mqt, mk, mvt)


def _out_ffn_kernel(x_ref, on_ref, om_ref, u_ref, uh_ref, pw_ref, ps_ref, won_ref, wom_ref, wop_ref, g2_ref,
                    w1_ref, w2_ref, gf_ref, y_ref, *, tm, seq, final):
    i = pl.program_id(0)
    t0 = (i * tm) % seq
    u = u_ref[...]
    halo = jnp.where(t0 == 0, 0.0, uh_ref[...])
    ext = jnp.concatenate([halo, u], axis=0)
    s2 = ext + pltpu.roll(ext, 1, axis=0)
    s4 = s2 + pltpu.roll(s2, 2, axis=0)
    s8 = s4 + pltpu.roll(s4, 4, axis=0)
    s16 = s8 + pltpu.roll(s8, 8, axis=0)
    sl = slice(POOL_HALO, POOL_HALO + tm)
    lane = lax.broadcasted_iota(jnp.int32, (tm, POOL_WIDTH), 1)
    t = t0 + lax.broadcasted_iota(jnp.int32, (tm, POOL_WIDTH), 0)
    sums = jnp.where(lane < 64, s2[sl], jnp.where(lane < 128, s4[sl], jnp.where(lane < 192, s8[sl], s16[sl])))
    win = jnp.where(lane < 64, 2, jnp.where(lane < 128, 4, jnp.where(lane < 192, 8, 16)))
    cnt = jnp.minimum(t + 1, win).astype(F32)
    pooled = (sums / cnt - u).astype(BF16)
    y_pool = (_dot(pooled, pw_ref[...]) * ps_ref[...]).astype(BF16)
    mix = _dot(on_ref[...], won_ref[...]) + _dot(om_ref[...], wom_ref[...]) + _dot(y_pool, wop_ref[...])
    x = x_ref[...] + mix
    h = _rms(x, g2_ref[...]).astype(BF16)
    ff = None
    for c in range(D_FF // FF_CHUNK):
        cs = slice(c * FF_CHUNK, (c + 1) * FF_CHUNK)
        a = jnp.maximum(_dot(h, w1_ref[:, cs]), 0.0)
        part = _dot((a * a).astype(BF16), w2_ref[cs, :])
        ff = part if ff is None else ff + part
    acc = x + ff
    if final:
        acc = _rms(acc, gf_ref[...])
    y_ref[...] = acc


def _out_ffn(x, o_nsa, o_mla, u, pw, ps, won, wom, wop, g2, w1, w2, gf, seq, final):
    n = x.shape[0]
    tm = ROW_TILE
    hb = tm // POOL_HALO
    row = lambda width: pl.BlockSpec((tm, width), lambda i: (i, 0))
    return pl.pallas_call(
        functools.partial(_out_ffn_kernel, tm=tm, seq=seq, final=final),
        grid=(n // tm,),
        in_specs=[row(D_MODEL), row(o_nsa.shape[1]), row(o_mla.shape[1]), row(POOL_WIDTH),
                  pl.BlockSpec((POOL_HALO, POOL_WIDTH), lambda i: (jnp.maximum(i * hb - 1, 0), 0)),
                  _const_spec(pw.shape), _const_spec(ps.shape), _const_spec(won.shape), _const_spec(wom.shape),
                  _const_spec(wop.shape), _const_spec(g2.shape), _const_spec(w1.shape), _const_spec(w2.shape),
                  _const_spec(gf.shape)],
        out_specs=row(D_MODEL),
        out_shape=jax.ShapeDtypeStruct((n, D_MODEL), F32),
        compiler_params=_params(("parallel",)),
        name="out_ffn",
    )(x, o_nsa, o_mla, u, u, pw, ps, won, wom, wop, g2, w1, w2, gf)


def _overlap_t(seq):
    ncp = seq // NSA_CMP_STRIDE
    n_slc = seq // NSA_SLC_LEN
    start = np.arange(ncp - 1)[None, :] * NSA_CMP_STRIDE
    end = start + NSA_CMP_LEN
    s0 = np.arange(n_slc)[:, None] * NSA_SLC_LEN
    s1 = s0 + NSA_SLC_LEN
    ov = np.clip(np.minimum(end, s1) - np.maximum(start, s0), 0, None) / NSA_CMP_LEN
    out = np.zeros((LANES, ncp), np.float32)
    out[:n_slc, :ncp - 1] = ov
    return jnp.asarray(out, BF16)


def kernel(x, positions, ln1_g, w_in, nsa_cmp_pos, nsa_cmp_w1, nsa_cmp_w2, mla_q_norm, mla_w_qup, mla_kv_norm,
           mla_w_kvup, pool_w, pool_scale, w_out, ln2_g, w_ff1, w_ff2, final_g):
    batch, seq, _ = x.shape
    depth = w_in.shape[0]
    n = batch * seq
    assert n % min(ROPE_TILE, n) == 0 and n % ROW_TILE == 0
    assert seq % ROW_TILE == 0 and seq % NSA_TILE == 0 and seq % MLA_TILE == 0
    assert seq // NSA_SLC_LEN <= NSA_DH and NSA_WINDOW % NSA_TILE == 0 and NSA_TILE % NSA_SLC_LEN == 0

    cos_t, sin_t = _rope_tables(positions)
    ovt = _overlap_t(seq)
    in_layout, in_layout_t = _in_layout(), _in_layout_t()
    lq, lqr, lk, lv = _mla_layouts()
    won_layout = _layout([r for g in range(NSA_KV_HEADS)
                          for r in ((g * NSA_HPG * NSA_DH, NSA_HPG * NSA_DH, 1.0), (None, NSA_DH, 0.0))])
    row2 = lambda v: v.reshape(1, -1)

    xf = x.reshape(n, D_MODEL)
    for l in range(depth):
        w_in_t, w_qup_t, w_kvup_t = w_in[l].T, mla_w_qup[l].T, mla_w_kvup[l].T
        w_all, wt_all = _take_rows(w_in_t, in_layout), _take_rows(w_in_t, in_layout_t)
        wqt, wqrt = _take_rows(w_qup_t, lq), _take_rows(w_qup_t, lqr)
        wk, wvt = _take_rows(w_kvup_t, lk), _take_rows(w_kvup_t, lv)
        q6t, vst, vwt, gate_t, kvc, kvs, kvw, mqt, mk, mvt, u = _proj_in(
            xf, row2(ln1_g[l]), w_all, wt_all, cos_t, sin_t, row2(mla_q_norm[l]), row2(mla_kv_norm[l]),
            wqt, wqrt, wk, wvt, seq)

        w1c = _blockdiag2(nsa_cmp_w1[l, 0], nsa_cmp_w1[l, 1]).astype(BF16)
        w2c = _blockdiag2(nsa_cmp_w2[l, 0], nsa_cmp_w2[l, 1]).astype(BF16)
        pos = jnp.concatenate([nsa_cmp_pos[l, 0], nsa_cmp_pos[l, 1]], axis=-1)[:, None, :]
        half = NSA_CMP_STRIDE
        kvcmp, vct = _compress(kvc, pos[:half], pos[half:], w1c[:half], w1c[half:], w2c, batch, seq)

        o_nsa = _nsa_attention(q6t, kvcmp, vct, kvs, vst, kvw, vwt, gate_t, ovt, batch, seq)
        o_mla = _mla_attention(mqt, mk, mvt, batch, seq)

        pw = _blockdiag2(_blockdiag2(pool_w[l, 0], pool_w[l, 1]), _blockdiag2(pool_w[l, 2], pool_w[l, 3])).astype(BF16)
        won = _take_rows(w_out[l], won_layout)
        wom = w_out[l, D_MIX_NSA:D_MIX_NSA + D_MIX_MLA].astype(BF16)
        wop = w_out[l, D_MIX_NSA + D_MIX_MLA:].astype(BF16)
        xf = _out_ffn(xf, o_nsa, o_mla, u, pw, row2(pool_scale[l]), won, wom, wop, row2(ln2_g[l]),
                      w_ff1[l].astype(BF16), w_ff2[l].astype(BF16), row2(final_g), seq, final=(l == depth - 1))
    return xf.reshape(batch, seq, D_MODEL)
```

```python
import functools

import numpy as np
import jax
import jax.numpy as jnp
from jax import lax
from jax.experimental import pallas as pl
from jax.experimental.pallas import tpu as pltpu

F32 = jnp.float32
BF16 = jnp.bfloat16

D_MODEL = 1024
NSA_HEADS = 6
NSA_KV_HEADS = 2
NSA_HPG = 3
NSA_DH = 64
NSA_CMP_LEN = 32
NSA_CMP_STRIDE = 16
NSA_SLC_LEN = 64
NSA_TOPN = 16
NSA_WINDOW = 512
NSA_FORCE = 1.0e4
MLA_HEADS = 6
MLA_NOPE = 64
MLA_ROPE = 32
MLA_VDIM = 64
V_ROWS = 80
ROPE_THETA = 10000.0
POOL_WIDTH = 256
POOL_HALO = 16
D_FF = 4096
EPS = 1e-6

COL_NSA_KV = NSA_HEADS * NSA_DH
COL_GATE = COL_NSA_KV + 3 * 2 * NSA_KV_HEADS * NSA_DH
COL_CQ = COL_GATE + 3 * NSA_HEADS
MLA_Q_LORA = 256
MLA_KV_LORA = 128
COL_CKV = COL_CQ + MLA_Q_LORA
COL_KR = COL_CKV + MLA_KV_LORA
COL_POOL = COL_KR + MLA_ROPE
D_MIX_NSA = NSA_HEADS * NSA_DH
D_MIX_MLA = MLA_HEADS * MLA_VDIM

LANES = 128
SUBLANES = 8
VMEM_LIMIT = 56 * 1024 * 1024

MASKED = -1.0e30
M_FLOOR = -1.0e20
LOG2E = 1.4426950408889634

ROW_TILE = 512
ROPE_TILE = 4096
NSA_TILE = 256
MLA_TILE = 512
FF_CHUNK = 1024
SELECT_VARIANTS = 4


def _dot(a, b):
    return jnp.dot(a, b, preferred_element_type=F32)


def _dot_nt(a, b):
    return lax.dot_general(a, b, (((1,), (1,)), ((), ())), preferred_element_type=F32)


def _rms(x, g):
    return x * lax.rsqrt(jnp.mean(x * x, axis=-1, keepdims=True) + EPS) * g


def _layout(spec):
    return tuple(spec)


def _nkv_col(br, kvi, g):
    return COL_NSA_KV + (br * 2 + kvi) * NSA_KV_HEADS * NSA_DH + g * NSA_DH


def _in_layout():
    spec = []
    for br in range(3):
        for g in range(2):
            spec += [(_nkv_col(br, 0, g), 64, 1.0)]
            spec += [(None, 64, 0.0)] if br == 1 else [(_nkv_col(br, 1, g), 64, 1.0)]
    spec += [(COL_CQ, MLA_Q_LORA, 1.0), (COL_CKV, MLA_KV_LORA, 1.0)]
    spec += [(COL_POOL, POOL_WIDTH, 1.0)]
    return _layout(spec)


def _in_layout_t():
    spec = []
    spec += [(0, NSA_HEADS * NSA_DH, 0.125)]
    for br in (1, 2):
        for g in range(2):
            spec += [(_nkv_col(br, 1, g), 64, 1.0), (None, V_ROWS - 64, 0.0)]
    for g in range(2):
        spec += [(COL_GATE + g * 3 * NSA_HPG, 3 * NSA_HPG, 1.0), (None, LANES - 3 * NSA_HPG, 0.0)]
    half = MLA_ROPE // 2
    spec += [(None, MLA_NOPE, 0.0), (COL_KR, MLA_ROPE, 1.0), (None, LANES - MLA_NOPE - MLA_ROPE, 0.0)]
    spec += [(None, MLA_NOPE, 0.0), (COL_KR + half, half, -1.0), (COL_KR, half, 1.0),
             (None, LANES - MLA_NOPE - MLA_ROPE, 0.0)]
    return _layout(spec)


def _mla_layouts():
    q, qr, k, v = [], [], [], []
    for h in range(MLA_HEADS):
        b = h * 96
        q += [(b, 96, 1.0), (None, 32, 0.0)]
        qr += [(None, 64, 0.0), (b + 80, 16, -1.0), (b + 64, 16, 1.0), (None, 32, 0.0)]
        k += [(h * 128, 64, 1.0), (None, 64, 0.0)]
        v += [(h * 128 + 64, 64, 1.0), (None, V_ROWS - 64, 0.0)]
    return _layout(q), _layout(qr), _layout(k), _layout(v)


def _take_rows(w, layout):
    cols = w.shape[1]
    pieces = [jnp.zeros((n, cols), w.dtype) if start is None else
              (w[start:start + n] if scale == 1.0 else w[start:start + n] * scale)
              for start, n, scale in layout]
    return jnp.concatenate(pieces, axis=0).astype(BF16)


def _blockdiag2(a, b):
    z = jnp.zeros_like(a)
    top = jnp.concatenate([a, z], axis=-1)
    bot = jnp.concatenate([z, b], axis=-1)
    return jnp.concatenate([top, bot], axis=-2)


def _const_spec(shape):
    nd = len(shape)
    return pl.BlockSpec(shape, lambda *_: (0,) * nd, pipeline_mode=pl.Buffered(1))


def _params(sem):
    return pltpu.CompilerParams(dimension_semantics=sem, vmem_limit_bytes=VMEM_LIMIT)


def _rope_kernel(pos_ref, inv_ref, cos_ref, sin_ref):
    ang = inv_ref[...] * pos_ref[...].astype(F32)
    cos_ref[...] = jnp.cos(ang)
    sin_ref[...] = jnp.sin(ang)


def _rope_tables(positions):
    n = positions.size
    half = MLA_ROPE // 2
    inv = jnp.power(jnp.float32(ROPE_THETA), -(jnp.arange(half, dtype=F32) / half))
    tm = min(ROPE_TILE, n)
    out = jax.ShapeDtypeStruct((half, n), F32)
    return pl.pallas_call(
        _rope_kernel,
        grid=(n // tm,),
        in_specs=[pl.BlockSpec((1, tm), lambda i: (0, i)), _const_spec((half, 1))],
        out_specs=[pl.BlockSpec((half, tm), lambda i: (0, i))] * 2,
        out_shape=[out, out],
        compiler_params=_params(("parallel",)),
        name="rope_tables",
    )(positions.reshape(1, n), inv.reshape(half, 1))


def _ones_rows(rows, cols):
    r = lax.broadcasted_iota(jnp.int32, (rows, cols), 0) % V_ROWS
    return jnp.where(r >= 64, 1.0, 0.0)


def _proj_in_kernel(x_ref, g_ref, w_ref, wt_ref, cost_ref, sint_ref, qg_ref, kvg_ref,
                    wqt_ref, wqrt_ref, wk_ref, wvt_ref,
                    q6t_ref, vst_ref, vwt_ref, gatet_ref, kvc_ref, kvs_ref, kvw_ref, mqt_ref, mk_ref, mvt_ref, u_ref,
                    *, seq):
    h = _rms(x_ref[...], g_ref[...]).astype(BF16)
    zt = _dot_nt(wt_ref[...], h)
    tm = h.shape[0]
    nv = NSA_KV_HEADS * V_ROWS
    nq = NSA_HEADS * NSA_DH
    q6t_ref[...] = (zt[0:nq] * LOG2E).astype(BF16)
    vst_ref[...] = (zt[nq:nq + nv] + _ones_rows(nv, tm)).astype(BF16)
    vwt_ref[...] = (zt[nq + nv:nq + 2 * nv] + _ones_rows(nv, tm)).astype(BF16)
    r0 = nq + 2 * nv
    gatet_ref[...] = jax.nn.sigmoid(zt[r0:r0 + NSA_KV_HEADS * LANES])
    kr_t = zt[r0 + NSA_KV_HEADS * LANES:r0 + (NSA_KV_HEADS + 1) * LANES]
    krrot_t = zt[r0 + (NSA_KV_HEADS + 1) * LANES:r0 + (NSA_KV_HEADS + 2) * LANES]
    kvw = NSA_KV_HEADS * LANES
    z = _dot_nt(h, w_ref[0:3 * kvw, :])
    kvc_ref[...] = z[:, 0:kvw]
    tok = (pl.program_id(0) * tm) % seq + lax.broadcasted_iota(jnp.int32, (tm, 2 * LANES), 0)
    lane = lax.broadcasted_iota(jnp.int32, (tm, 2 * LANES), 1) & (LANES - 1)
    onehot = jnp.where(lane - NSA_DH == tok // NSA_SLC_LEN, 1.0, 0.0)
    kvs_ref[...] = (z[:, kvw:2 * kvw] + onehot).astype(BF16)
    kvw_ref[...] = z[:, 2 * kvw:3 * kvw].astype(BF16)
    lat = MLA_Q_LORA + MLA_KV_LORA
    z = _dot_nt(h, w_ref[3 * kvw:3 * kvw + lat + POOL_WIDTH, :])
    u_ref[...] = z[:, lat:lat + POOL_WIDTH]
    cqn = _rms(z[:, 0:MLA_Q_LORA], qg_ref[...]).astype(BF16)
    qa = _dot_nt(wqt_ref[...], cqn)
    qb = _dot_nt(wqrt_ref[...], cqn)
    c16, s16 = cost_ref[...], sint_ref[...]
    cos_t = jnp.concatenate([jnp.ones((64, tm), F32), c16, c16, jnp.ones((32, tm), F32)], axis=0)
    sin_t = jnp.concatenate([jnp.zeros((64, tm), F32), s16, s16, jnp.zeros((32, tm), F32)], axis=0)
    scale = (MLA_NOPE + MLA_ROPE) ** -0.5 * LOG2E
    for hh in range(MLA_HEADS):
        sl = slice(hh * LANES, (hh + 1) * LANES)
        mqt_ref[sl, :] = ((qa[sl] * cos_t + qb[sl] * sin_t) * scale).astype(BF16)
    ckvn = _rms(z[:, MLA_Q_LORA:lat], kvg_ref[...]).astype(BF16)
    k_rope = (kr_t * cos_t + krrot_t * sin_t).T
    ka = _dot_nt(ckvn, wk_ref[...])
    for hh in range(MLA_HEADS):
        sl = slice(hh * LANES, (hh + 1) * LANES)
        mk_ref[:, sl] = (ka[:, sl] + k_rope).astype(BF16)
    mvt_ref[...] = (_dot_nt(wvt_ref[...], ckvn) + _ones_rows(MLA_HEADS * V_ROWS, tm)).astype(BF16)


def _proj_in(x, g, w_all, wt_all, cos_t, sin_t, qg, kvg, wqt, wqrt, wk, wvt, seq):
    n = x.shape[0]
    tm = ROW_TILE
    row = lambda width: pl.BlockSpec((tm, width), lambda i: (i, 0))
    col = lambda height: pl.BlockSpec((height, tm), lambda i: (0, i))
    nv = NSA_KV_HEADS * V_ROWS
    outs = [("t", NSA_HEADS * NSA_DH, BF16), ("t", nv, BF16), ("t", nv, BF16), ("t", 256, F32), ("r", 256, F32),
            ("r", 256, BF16), ("r", 256, BF16), ("t", 768, BF16), ("r", 768, BF16), ("t", MLA_HEADS * V_ROWS, BF16),
            ("r", 256, F32)]
    return pl.pallas_call(
        functools.partial(_proj_in_kernel, seq=seq),
        grid=(n // tm,),
        in_specs=[row(D_MODEL), _const_spec(g.shape), _const_spec(w_all.shape), _const_spec(wt_all.shape),
                  col(MLA_ROPE // 2), col(MLA_ROPE // 2), _const_spec(qg.shape), _const_spec(kvg.shape),
                  _const_spec(wqt.shape), _const_spec(wqrt.shape), _const_spec(wk.shape), _const_spec(wvt.shape)],
        out_specs=[col(w) if kind == "t" else row(w) for kind, w, _ in outs],
        out_shape=[jax.ShapeDtypeStruct((w, n) if kind == "t" else (n, w), dt) for kind, w, dt in outs],
        compiler_params=_params(("parallel",)),
        name="proj_in",
    )(x, g, w_all, wt_all, cos_t, sin_t, qg, kvg, wqt, wqrt, wk, wvt)


def _compress_kernel(kvc_ref, posa_ref, posb_ref, wa_ref, wb_ref, w2_ref, o_ref, vt_ref, *, ncp):
    a = jnp.zeros((ncp, LANES), F32)
    b = jnp.zeros((ncp, LANES), F32)
    for l in range(NSA_CMP_STRIDE):
        xl = kvc_ref[pl.ds(l, ncp, stride=NSA_CMP_STRIDE), :]
        a = a + _dot((xl + posa_ref[l]).astype(BF16), wa_ref[l])
        b = b + _dot((xl + posb_ref[l]).astype(BF16), wb_ref[l])
    pre = a + pltpu.roll(b, ncp - 1, axis=0)
    hid = jax.nn.gelu(pre).astype(BF16)
    out = _dot(hid, w2_ref[...])
    o_ref[...] = out.astype(BF16)
    ones = jnp.ones((V_ROWS - NSA_DH, ncp), F32)
    vt_ref[...] = jnp.concatenate([out.T[NSA_DH:2 * NSA_DH], ones], axis=0).astype(BF16)


def _compress(kvc, posa, posb, wa, wb, w2, batch, seq):
    ncp = seq // NSA_CMP_STRIDE
    return pl.pallas_call(
        functools.partial(_compress_kernel, ncp=ncp),
        grid=(batch, NSA_KV_HEADS),
        in_specs=[pl.BlockSpec((seq, LANES), lambda b, g: (b, g)), _const_spec(posa.shape), _const_spec(posb.shape),
                  _const_spec(wa.shape), _const_spec(wb.shape), _const_spec(w2.shape)],
        out_specs=[pl.BlockSpec((ncp, LANES), lambda b, g: (b, g)),
                   pl.BlockSpec((V_ROWS, ncp), lambda b, g: (b * NSA_KV_HEADS + g, 0))],
        out_shape=[jax.ShapeDtypeStruct((batch * ncp, 2 * LANES), BF16),
                   jax.ShapeDtypeStruct((batch * NSA_KV_HEADS * V_ROWS, ncp), BF16)],
        compiler_params=_params(("parallel", "parallel")),
        name="nsa_compress",
    )(kvc, posa, posb, wa, wb, w2)


def _flash_update(buf, v_t, m, acc_ref, mask=None):
    load = (lambda: buf[...]) if mask is None else (lambda: jnp.where(mask, buf[...], MASKED))
    m_new = jnp.maximum(m, jnp.max(load(), axis=0, keepdims=True))
    alpha = jnp.exp2(m - m_new)
    p = jnp.exp2(load() - m_new)
    acc_ref[...] = alpha * acc_ref[...] + _dot(v_t, p.astype(BF16))
    return m_new


class _FlashChain:
    def __init__(self, first, last, scores, values, bufs, acc_ref, width, unroll=2):
        self.first, self.last, self.scores, self.values = first, last, scores, values
        self.bufs, self.acc_ref, self.width, self.unroll = bufs, acc_ref, width, unroll
        self.m = None

    def start(self, first_mask=None):
        s_first = self.scores(self.first)
        self.bufs[0][...] = s_first if first_mask is None else jnp.where(first_mask, s_first, MASKED)
        self.acc_ref[...] = jnp.zeros(self.acc_ref.shape, F32)
        self.m = jnp.full((1, self.width), M_FLOOR, F32)

    def loops(self):
        bufs, acc_ref, scores, values = self.bufs, self.acc_ref, self.scores, self.values

        def steps(count, t0):
            def body(jj, m):
                for k in range(count):
                    t = t0 + count * jj + k
                    cur = bufs[k % 2]
                    if count > 1:
                        bufs[(k + 1) % 2][...] = scores(t + 1)
                    m = _flash_update(cur, values(t), m, acc_ref)
                    if count == 1:
                        cur[...] = scores(t + 1)
                return m
            return body

        t0, rem, level = self.first, self.last - self.first, self.unroll
        while level >= 1:
            trips = rem // level
            self.m = lax.fori_loop(0, trips, steps(level, t0), self.m)
            t0, rem, level = t0 + trips * level, rem - trips * level, level // 2

    def finish(self, last_mask):
        _flash_update(self.bufs[0], self.values(self.last), self.m, self.acc_ref, mask=last_mask)
        return self.acc_ref[0:64] * (1.0 / self.acc_ref[64:65])


def _nsa_select(q3, q0, kvc_ref, vct_ref, ovt_ref, ocmp_ref, bias_ref, *, rows, ngroups, tq, n_slc, n_sel):
    width = NSA_HPG * tq
    ncp = kvc_ref.shape[0]
    s = _dot(kvc_ref[0:rows, :], q3)
    nblk = lax.broadcasted_iota(jnp.int32, (rows, width), 0)
    qpos = q0 + (lax.broadcasted_iota(jnp.int32, (1, width), 1) & (tq - 1))
    last_blk = (qpos - (NSA_CMP_LEN - 1)) >> 4
    s = jnp.where(nblk <= last_blk, s, -jnp.inf)
    m = jnp.max(s, axis=0, keepdims=True)
    m = jnp.where(m == -jnp.inf, 0.0, m)
    e = jnp.exp2(s - m)
    d = jnp.sum(e, axis=0, keepdims=True)
    p = e * (1.0 / jnp.where(d > 0.0, d, 1.0))

    def pad(a):
        return a if rows == ncp else jnp.concatenate([a, jnp.zeros((ncp - rows, a.shape[1]), a.dtype)], axis=0)

    ocmp_ref[...] = _dot(vct_ref[0:NSA_DH, :], pad(p.astype(BF16)))

    every_valid_block_selected = q0 + tq <= n_sel * NSA_SLC_LEN

    @pl.when(every_valid_block_selected)
    def _():
        bias_ref[...] = jnp.zeros(bias_ref.shape, BF16)

    @pl.when(jnp.logical_not(every_valid_block_selected))
    def _():
        psum = p[:, 0:tq] + p[:, tq:2 * tq] + p[:, 2 * tq:3 * tq]
        hi = psum.astype(BF16)
        lo = (psum - hi.astype(F32)).astype(BF16)
        ovt = ovt_ref[...]
        imp = _dot(ovt, pad(hi)) + _dot(ovt, pad(lo))
        nblocks = ngroups * SUBLANES
        blk = lax.broadcasted_iota(jnp.int32, (LANES, tq), 0)
        qp = q0 + lax.broadcasted_iota(jnp.int32, (LANES, tq), 1)
        cur = qp >> 6
        forced = (blk == 0) | (blk == cur) | (blk == cur - 1)
        valid = (blk << 6) <= qp
        score = jnp.where(forced, NSA_FORCE, jnp.where(valid, imp, -1.0))[0:nblocks]
        sub = SUBLANES
        sidx = lax.broadcasted_iota(jnp.int32, (sub, tq), 0)
        groups = [score[k * sub:(k + 1) * sub] for k in range(ngroups)]
        cnts = [jnp.zeros((sub, tq), F32) for _ in groups]
        for sp in range(nblocks):
            col = score[sp:sp + 1, :]
            for k, grp in enumerate(groups):
                if k * sub > sp:
                    cnts[k] = cnts[k] + jnp.where(col >= grp, 1.0, 0.0)
                elif (k + 1) * sub - 1 <= sp:
                    cnts[k] = cnts[k] + jnp.where(col > grp, 1.0, 0.0)
                else:
                    after = jnp.where(col >= grp, 1.0, 0.0)
                    cnts[k] = cnts[k] + jnp.where(sidx + k * sub > sp, after, jnp.where(col > grp, 1.0, 0.0))
        bias = jnp.where(jnp.concatenate(cnts, axis=0) < n_sel, 0.0, MASKED)
        if nblocks < NSA_DH:
            bias = jnp.concatenate([bias, jnp.zeros((NSA_DH - nblocks, tq), F32)], axis=0)
        bias_ref[...] = bias.astype(BF16)


def _nsa_kernel(qt_ref, kvc_ref, vct_ref, kvs_ref, vst_ref, kvw_ref, vwt_ref, gt_ref, ovt_ref, o_ref,
                bias_sc, ocmp_sc, acc_sc, sa_sc, sb_sc, wacc_sc, wa_sc, wb_sc, *, tq, n_slc, n_sel):
    nq = qt_ref.shape[1] // tq

    def q_tile(i, carry):
        _nsa_q_tile(i, qt_ref, kvc_ref, vct_ref, kvs_ref, vst_ref, kvw_ref, vwt_ref, gt_ref, ovt_ref, o_ref,
                    bias_sc, ocmp_sc, acc_sc, sa_sc, sb_sc, wacc_sc, wa_sc, wb_sc, tq=tq, n_slc=n_slc, n_sel=n_sel)
        return carry

    lax.fori_loop(0, nq, q_tile, 0)


def _nsa_q_tile(i, qt_ref, kvc_ref, vct_ref, kvs_ref, vst_ref, kvw_ref, vwt_ref, gt_ref, ovt_ref, o_ref,
                bias_sc, ocmp_sc, acc_sc, sa_sc, sb_sc, wacc_sc, wa_sc, wb_sc, *, tq, n_slc, n_sel):
    q0 = pl.multiple_of(i * tq, tq)
    width = NSA_HPG * tq
    q_top = jnp.concatenate([qt_ref[h * NSA_DH:(h + 1) * NSA_DH, pl.ds(q0, tq)] for h in range(NSA_HPG)],
                            axis=1)
    q3 = jnp.concatenate([q_top, jnp.zeros((NSA_DH, width), BF16)], axis=0)

    ncp = kvc_ref.shape[0]
    nq = ncp * NSA_CMP_STRIDE // tq
    divisible = all(v % SELECT_VARIANTS == 0 for v in (nq, ncp // 16, n_slc // SUBLANES))
    nvar = SELECT_VARIANTS if divisible else 1
    for c in range(nvar):
        @pl.when((i * nvar) // nq == c)
        def _(c=c):
            _nsa_select(q3, q0, kvc_ref, vct_ref, ovt_ref, ocmp_sc, bias_sc, rows=ncp * (c + 1) // nvar,
                        ngroups=(n_slc // SUBLANES) * (c + 1) // nvar, tq=tq, n_slc=n_slc, n_sel=n_sel)
    o_cmp = ocmp_sc[...]

    cc = lax.broadcasted_iota(jnp.int32, (tq, width), 0)
    rr = lax.broadcasted_iota(jnp.int32, (tq, width), 1) & (tq - 1)

    bias = bias_sc[...]
    q3_slc = jnp.concatenate([q_top, jnp.concatenate([bias] * NSA_HPG, axis=1)], axis=0)

    def slc_scores(t):
        return _dot(kvs_ref[pl.ds(pl.multiple_of(t * tq, tq), tq), :], q3_slc)

    def slc_values(t):
        return vst_ref[:, pl.ds(pl.multiple_of(t * tq, tq), tq)]

    def win_scores(t):
        return _dot(kvw_ref[pl.ds(pl.multiple_of(t * tq, tq), tq), :], q3)

    def win_values(t):
        return vwt_ref[:, pl.ds(pl.multiple_of(t * tq, tq), tq)]

    back = NSA_WINDOW // tq
    slc = _FlashChain(0, i, slc_scores, slc_values, (sa_sc, sb_sc), acc_sc, width, unroll=8)
    win = _FlashChain(jnp.maximum(i - back, 0), i, win_scores, win_values, (wa_sc, wb_sc), wacc_sc, width)
    slc.start()
    win.start(first_mask=jnp.logical_or(cc > rr, i < back))
    slc.loops()
    win.loops()
    o_slc = slc.finish(cc <= rr)
    o_win = win.finish(cc <= rr)

    gt = gt_ref[:, pl.ds(q0, tq)]
    outs = []
    for h in range(NSA_HPG):
        sl = slice(h * tq, (h + 1) * tq)
        outs.append(gt[3 * h:3 * h + 1] * o_cmp[:, sl] + gt[3 * h + 1:3 * h + 2] * o_slc[:, sl]
                    + gt[3 * h + 2:3 * h + 3] * o_win[:, sl])
    outs.append(jnp.zeros((NSA_DH, tq), F32))
    o_ref[pl.ds(q0, tq), :] = jnp.concatenate(outs, axis=0).T.astype(BF16)


def _nsa_attention(q6t, kvcmp, vct, kvs, vst, kvw, vwt, gate_t, ovt, batch, seq):
    tq = NSA_TILE
    ncp = seq // NSA_CMP_STRIDE
    n_slc = seq // NSA_SLC_LEN
    width = NSA_HPG * tq
    kern = functools.partial(_nsa_kernel, tq=tq, n_slc=n_slc, n_sel=min(NSA_TOPN, n_slc))
    kv_spec = pl.BlockSpec((seq, LANES), lambda b, g: (b, g))
    vt_spec = pl.BlockSpec((V_ROWS, seq), lambda b, g: (g, b))
    return pl.pallas_call(
        kern,
        grid=(batch, NSA_KV_HEADS),
        in_specs=[pl.BlockSpec((NSA_HPG * NSA_DH, seq), lambda b, g: (g, b)),
                  pl.BlockSpec((ncp, LANES), lambda b, g: (b, g)),
                  pl.BlockSpec((V_ROWS, ncp), lambda b, g: (b * NSA_KV_HEADS + g, 0)),
                  kv_spec, vt_spec, kv_spec, vt_spec,
                  pl.BlockSpec((LANES, seq), lambda b, g: (g, b)),
                  _const_spec(ovt.shape)],
        out_specs=pl.BlockSpec((seq, 2 * LANES), lambda b, g: (b, g)),
        out_shape=jax.ShapeDtypeStruct((batch * seq, NSA_KV_HEADS * 2 * LANES), BF16),
        scratch_shapes=[pltpu.VMEM((NSA_DH, tq), BF16), pltpu.VMEM((NSA_DH, width), F32),
                        pltpu.VMEM((V_ROWS, width), F32), pltpu.VMEM((tq, width), F32), pltpu.VMEM((tq, width), F32),
                        pltpu.VMEM((V_ROWS, width), F32), pltpu.VMEM((tq, width), F32), pltpu.VMEM((tq, width), F32)],
        compiler_params=_params(("parallel", "parallel")),
        name="nsa_attention",
    )(q6t, kvcmp, vct, kvs, vst, kvw, vwt, gate_t, ovt)


def _mla_kernel(qt_ref, k_ref, vt_ref, o_ref, *scratch, tq):
    nq = qt_ref.shape[1] // tq
    causal = lax.broadcasted_iota(jnp.int32, (tq, tq), 0) <= lax.broadcasted_iota(jnp.int32, (tq, tq), 1)
    sets = [scratch[3 * c:3 * c + 3] for c in range(4)]

    def tile_chains(i, sets2):
        q0 = pl.multiple_of(i * tq, tq)
        chains = []
        for hh, (acc_sc, sa_sc, sb_sc) in zip(range(2), sets2):
            def scores(t, hh=hh):
                off = pl.multiple_of(t * tq, tq)
                return _dot(k_ref[pl.ds(off, tq), hh * LANES:(hh + 1) * LANES],
                            qt_ref[hh * LANES:(hh + 1) * LANES, pl.ds(q0, tq)])

            def values(t, hh=hh):
                return vt_ref[hh * V_ROWS:(hh + 1) * V_ROWS, pl.ds(pl.multiple_of(t * tq, tq), tq)]

            chains.append(_FlashChain(0, i, scores, values, (sa_sc, sb_sc), acc_sc, tq, unroll=4))
        return q0, chains

    def tile_pair(p, carry):
        tiles = [tile_chains(p, sets[0:2]), tile_chains(nq - 1 - p, sets[2:4])]
        chains = [c for _, cs in tiles for c in cs]
        for chain in chains:
            chain.start()
        for chain in chains:
            chain.loops()
        outs = [chain.finish(causal) for chain in chains]
        for k, (q0, _) in enumerate(tiles):
            o_ref[pl.ds(q0, tq), :] = jnp.concatenate(outs[2 * k:2 * k + 2], axis=0).T.astype(BF16)
        return carry

    lax.fori_loop(0, nq // 2, tile_pair, 0)


def _mla_attention(mqt, mk, mvt, batch, seq):
    tq = MLA_TILE
    pairs = MLA_HEADS // 2
    return pl.pallas_call(
        functools.partial(_mla_kernel, tq=tq),
        grid=(batch, pairs),
        in_specs=[pl.BlockSpec((2 * LANES, seq), lambda b, p: (p, b)),
                  pl.BlockSpec((seq, 2 * LANES), lambda b, p: (b, p)),
                  pl.BlockSpec((2 * V_ROWS, seq), lambda b, p: (p, b))],
        out_specs=pl.BlockSpec((seq, LANES), lambda b, p: (b, p)),
        out_shape=jax.ShapeDtypeStruct((batch * seq, pairs * LANES), BF16),
        scratch_shapes=[pltpu.VMEM((V_ROWS, tq), F32), pltpu.VMEM((tq, tq), F32), pltpu.VMEM((tq, tq), F32)] * 4,
        compiler_params=_params(("parallel", "parallel")),
        name="mla_attention",
    )(mqt, mk, mvt)


def _out_ffn_kernel(x_ref, on_ref, om_ref, u_ref, uh_ref, pw_ref, ps_ref, won_ref, wom_ref, wop_ref, g2_ref,
                    w1_ref, w2_ref, gf_ref, y_ref, *, tm, seq, final):
    i = pl.program_id(0)
    t0 = (i * tm) % seq
    u = u_ref[...]
    halo = jnp.where(t0 == 0, 0.0, uh_ref[...])
    ext = jnp.concatenate([halo, u], axis=0)
    s2 = ext + pltpu.roll(ext, 1, axis=0)
    s4 = s2 + pltpu.roll(s2, 2, axis=0)
    s8 = s4 + pltpu.roll(s4, 4, axis=0)
    s16 = s8 + pltpu.roll(s8, 8, axis=0)
    sl = slice(POOL_HALO, POOL_HALO + tm)
    lane = lax.broadcasted_iota(jnp.int32, (tm, POOL_WIDTH), 1)
    t = t0 + lax.broadcasted_iota(jnp.int32, (tm, POOL_WIDTH), 0)
    sums = jnp.where(lane < 64, s2[sl], jnp.where(lane < 128, s4[sl], jnp.where(lane < 192, s8[sl], s16[sl])))
    win = jnp.where(lane < 64, 2, jnp.where(lane < 128, 4, jnp.where(lane < 192, 8, 16)))
    cnt = jnp.minimum(t + 1, win).astype(F32)
    pooled = (sums / cnt - u).astype(BF16)
    y_pool = (_dot(pooled, pw_ref[...]) * ps_ref[...]).astype(BF16)
    mix = _dot(on_ref[...], won_ref[...]) + _dot(om_ref[...], wom_ref[...]) + _dot(y_pool, wop_ref[...])
    x = x_ref[...] + mix
    h = _rms(x, g2_ref[...]).astype(BF16)
    ff = None
    for c in range(D_FF // FF_CHUNK):
        cs = slice(c * FF_CHUNK, (c + 1) * FF_CHUNK)
        a = jnp.maximum(_dot(h, w1_ref[:, cs]), 0.0)
        part = _dot((a * a).astype(BF16), w2_ref[cs, :])
        ff = part if ff is None else ff + part
    acc = x + ff
    if final:
        acc = _rms(acc, gf_ref[...])
    y_ref[...] = acc


def _out_ffn(x, o_nsa, o_mla, u, pw, ps, won, wom, wop, g2, w1, w2, gf, seq, final):
    n = x.shape[0]
    tm = ROW_TILE
    hb = tm // POOL_HALO
    row = lambda width: pl.BlockSpec((tm, width), lambda i: (i, 0))
    return pl.pallas_call(
        functools.partial(_out_ffn_kernel, tm=tm, seq=seq, final=final),
        grid=(n // tm,),
        in_specs=[row(D_MODEL), row(o_nsa.shape[1]), row(o_mla.shape[1]), row(POOL_WIDTH),
                  pl.BlockSpec((POOL_HALO, POOL_WIDTH), lambda i: (jnp.maximum(i * hb - 1, 0), 0)),
                  _const_spec(pw.shape), _const_spec(ps.shape), _const_spec(won.shape), _const_spec(wom.shape),
                  _const_spec(wop.shape), _const_spec(g2.shape), _const_spec(w1.shape), _const_spec(w2.shape),
                  _const_spec(gf.shape)],
        out_specs=row(D_MODEL),
        out_shape=jax.ShapeDtypeStruct((n, D_MODEL), F32),
        compiler_params=_params(("parallel",)),
        name="out_ffn",
    )(x, o_nsa, o_mla, u, u, pw, ps, won, wom, wop, g2, w1, w2, gf)


def _overlap_t(seq):
    ncp = seq // NSA_CMP_STRIDE
    n_slc = seq // NSA_SLC_LEN
    start = np.arange(ncp - 1)[None, :] * NSA_CMP_STRIDE
    end = start + NSA_CMP_LEN
    s0 = np.arange(n_slc)[:, None] * NSA_SLC_LEN
    s1 = s0 + NSA_SLC_LEN
    ov = np.clip(np.minimum(end, s1) - np.maximum(start, s0), 0, None) / NSA_CMP_LEN
    out = np.zeros((LANES, ncp), np.float32)
    out[:n_slc, :ncp - 1] = ov
    return jnp.asarray(out, BF16)


def kernel(x, positions, ln1_g, w_in, nsa_cmp_pos, nsa_cmp_w1, nsa_cmp_w2, mla_q_norm, mla_w_qup, mla_kv_norm,
           mla_w_kvup, pool_w, pool_scale, w_out, ln2_g, w_ff1, w_ff2, final_g):
    batch, seq, _ = x.shape
    depth = w_in.shape[0]
    n = batch * seq
    assert n % min(ROPE_TILE, n) == 0 and n % ROW_TILE == 0
    assert seq % ROW_TILE == 0 and seq % NSA_TILE == 0 and seq % (2 * MLA_TILE) == 0
    assert seq // NSA_SLC_LEN <= NSA_DH and NSA_WINDOW % NSA_TILE == 0 and NSA_TILE % NSA_SLC_LEN == 0

    cos_t, sin_t = _rope_tables(positions)
    ovt = _overlap_t(seq)
    in_layout, in_layout_t = _in_layout(), _in_layout_t()
    lq, lqr, lk, lv = _mla_layouts()
    won_layout = _layout([r for g in range(NSA_KV_HEADS)
                          for r in ((g * NSA_HPG * NSA_DH, NSA_HPG * NSA_DH, 1.0), (None, NSA_DH, 0.0))])
    row2 = lambda v: v.reshape(1, -1)

    xf = x.reshape(n, D_MODEL)
    for l in range(depth):
        w_in_t, w_qup_t, w_kvup_t = w_in[l].T, mla_w_qup[l].T, mla_w_kvup[l].T
        w_all, wt_all = _take_rows(w_in_t, in_layout), _take_rows(w_in_t, in_layout_t)
        wqt, wqrt = _take_rows(w_qup_t, lq), _take_rows(w_qup_t, lqr)
        wk, wvt = _take_rows(w_kvup_t, lk), _take_rows(w_kvup_t, lv)
        q6t, vst, vwt, gate_t, kvc, kvs, kvw, mqt, mk, mvt, u = _proj_in(
            xf, row2(ln1_g[l]), w_all, wt_all, cos_t, sin_t, row2(mla_q_norm[l]), row2(mla_kv_norm[l]),
            wqt, wqrt, wk, wvt, seq)

        w1c = _blockdiag2(nsa_cmp_w1[l, 0], nsa_cmp_w1[l, 1]).astype(BF16)
        w2c = _blockdiag2(nsa_cmp_w2[l, 0], nsa_cmp_w2[l, 1]).astype(BF16)
        pos = jnp.concatenate([nsa_cmp_pos[l, 0], nsa_cmp_pos[l, 1]], axis=-1)[:, None, :]
        half = NSA_CMP_STRIDE
        kvcmp, vct = _compress(kvc, pos[:half], pos[half:], w1c[:half], w1c[half:], w2c, batch, seq)

        o_nsa = _nsa_attention(q6t, kvcmp, vct, kvs, vst, kvw, vwt, gate_t, ovt, batch, seq)
        o_mla = _mla_attention(mqt, mk, mvt, batch, seq)

        pw = _blockdiag2(_blockdiag2(pool_w[l, 0], pool_w[l, 1]), _blockdiag2(pool_w[l, 2], pool_w[l, 3])).astype(BF16)
        won = _take_rows(w_out[l], won_layout)
        wom = w_out[l, D_MIX_NSA:D_MIX_NSA + D_MIX_MLA].astype(BF16)
        wop = w_out[l, D_MIX_NSA + D_MIX_MLA:].astype(BF16)
        xf = _out_ffn(xf, o_nsa, o_mla, u, pw, row2(pool_scale[l]), won, wom, wop, row2(ln2_g[l]),
                      w_ff1[l].astype(BF16), w_ff2[l].astype(BF16), row2(final_g), seq, final=(l == depth - 1))
    return xf.reshape(batch, seq, D_MODEL)
```
